```python
import jax, jax.numpy as jnp
from jax import lax
import numpy as np

D_MODEL = 4096
BATCH = 8
SEQ = 2048
DEPTH = 2
DEC_BATCH = 2
DEC_SEQ = 4096
PAST_LEN = 128

N_MIXERS = 2
N_A_LAYERS = (DEPTH + 1) // 2
N_B_LAYERS = DEPTH // 2
CHUNK = 128
D_U = D_MODEL
A_HEADS = 32
A_HEAD_DIM = D_U // A_HEADS
B_GROUPS = 8
B_GROUP_DIM = D_MODEL // B_GROUPS
D_FF = -(-8 * D_MODEL // (3 * 256)) * 256
EPS = 1e-6

kernel_name = "hybrid_gmlp_fnet_encoder"


def rmsnorm(x, g):
    xf = x.astype(jnp.float32)
    r = lax.rsqrt(jnp.mean(xf * xf, axis=-1, keepdims=True) + EPS)
    return (xf * r * g.astype(jnp.float32)).astype(x.dtype)


def layernorm(x, g, b):
    xf = x.astype(jnp.float32)
    mu = jnp.mean(xf, axis=-1, keepdims=True)
    xc = xf - mu
    r = lax.rsqrt(jnp.mean(xc * xc, axis=-1, keepdims=True) + EPS)
    return (xc * r * g.astype(jnp.float32) + b.astype(jnp.float32)).astype(x.dtype)


def gmlp_mixer(h, w_in, ln_g, ln_b, w_s, b_s, w_out):
    B, S, _ = h.shape
    z = jax.nn.gelu(h @ w_in)
    u, v = jnp.split(z, 2, axis=-1)
    v = layernorm(v, ln_g, ln_b)
    v = v.reshape(B, S // CHUNK, CHUNK, A_HEADS, A_HEAD_DIM)
    s = jnp.einsum('hpq,bcqhd->bcphd', w_s, v) + jnp.transpose(b_s)[None, None, :, :, None]
    s = s.reshape(B, S, D_U)
    return (u * s) @ w_out


def fourier_mixer(h, w_out):
    B, S, D = h.shape
    hf = h.astype(jnp.float32).reshape(B, S, B_GROUPS, B_GROUP_DIM)
    y = jnp.fft.fft2(hf, axes=(1, 3), norm='ortho').real
    y = y.astype(h.dtype).reshape(B, S, D)
    return y @ w_out


def swiglu(h, w_gate, w_up, w_down):
    return (jax.nn.silu(h @ w_gate) * (h @ w_up)) @ w_down


def trunk(x, a_norm_g, a_w_in, a_ln_g, a_ln_b, a_w_s, a_b_s, a_w_out,
          b_norm_g, b_w_out, ffn_norm_g, ffn_w_gate, ffn_w_up, ffn_w_down, final_norm_g):
    for i in range(DEPTH):
        j = i // N_MIXERS
        if i % N_MIXERS == 0:
            x = x + gmlp_mixer(rmsnorm(x, a_norm_g[j]), a_w_in[j], a_ln_g[j], a_ln_b[j],
                               a_w_s[j], a_b_s[j], a_w_out[j])
        else:
            x = x + fourier_mixer(rmsnorm(x, b_norm_g[j]), b_w_out[j])
        x = x + swiglu(rmsnorm(x, ffn_norm_g[i]), ffn_w_gate[i], ffn_w_up[i], ffn_w_down[i])
    return rmsnorm(x, final_norm_g)


def setup_inputs(seed: int = 0) -> dict:
    key = jax.random.key(seed)
    ks = jax.random.split(key, 20)

    def nrm(k, shape, scale):
        return jax.random.normal(k, shape, jnp.float32) * scale

    return {
        "x_prompt": nrm(ks[0], (BATCH, SEQ, D_MODEL), 1.0),
        "x_sample": nrm(ks[1], (DEC_BATCH, DEC_SEQ, D_MODEL), 1.0),
        "a_norm_g": 1.0 + nrm(ks[2], (N_A_LAYERS, D_MODEL), 0.02),
        "a_w_in": nrm(ks[3], (N_A_LAYERS, D_MODEL, 2 * D_U), D_MODEL ** -0.5),
        "a_ln_g": 1.0 + nrm(ks[4], (N_A_LAYERS, D_U), 0.02),
        "a_ln_b": nrm(ks[5], (N_A_LAYERS, D_U), 0.02),
        "a_w_s": nrm(ks[6], (N_A_LAYERS, A_HEADS, CHUNK, CHUNK), CHUNK ** -0.5),
        "a_b_s": 1.0 + nrm(ks[7], (N_A_LAYERS, A_HEADS, CHUNK), 0.02),
        "a_w_out": nrm(ks[8], (N_A_LAYERS, D_U, D_MODEL), D_U ** -0.5),
        "b_norm_g": 1.0 + nrm(ks[9], (N_B_LAYERS, D_MODEL), 0.02),
        "b_w_out": nrm(ks[10], (N_B_LAYERS, D_MODEL, D_MODEL), D_MODEL ** -0.5),
        "ffn_norm_g": 1.0 + nrm(ks[11], (DEPTH, D_MODEL), 0.02),
        "ffn_w_gate": nrm(ks[12], (DEPTH, D_MODEL, D_FF), D_MODEL ** -0.5),
        "ffn_w_up": nrm(ks[13], (DEPTH, D_MODEL, D_FF), D_MODEL ** -0.5),
        "ffn_w_down": nrm(ks[14], (DEPTH, D_FF, D_MODEL), D_FF ** -0.5),
        "final_norm_g": 1.0 + nrm(ks[15], (D_MODEL,), 0.02),
    }


def reference(x_prompt, x_sample, a_norm_g, a_w_in, a_ln_g, a_ln_b, a_w_s, a_b_s, a_w_out,
              b_norm_g, b_w_out, ffn_norm_g, ffn_w_gate, ffn_w_up, ffn_w_down, final_norm_g):
    y_prompt = trunk(x_prompt, a_norm_g, a_w_in, a_ln_g, a_ln_b, a_w_s, a_b_s, a_w_out,
                     b_norm_g, b_w_out, ffn_norm_g, ffn_w_gate, ffn_w_up, ffn_w_down, final_norm_g)
    y_sample = trunk(x_sample, a_norm_g, a_w_in, a_ln_g, a_ln_b, a_w_s, a_b_s, a_w_out,
                     b_norm_g, b_w_out, ffn_norm_g, ffn_w_gate, ffn_w_up, ffn_w_down, final_norm_g)
    return (y_prompt, y_sample)
```

```python
import functools
import math

import jax
import jax.numpy as jnp
from jax import lax
from jax.experimental import pallas as pl
from jax.experimental.pallas import tpu as pltpu

EPS = 1e-6
B_GROUPS = 8

V7X_VMEM_BYTES = 64 * 1024 * 1024
VMEM_LIMIT_BYTES = V7X_VMEM_BYTES - 8 * 1024 * 1024

F32 = jnp.float32
BF16 = jnp.bfloat16


def _params(*semantics):
    return pltpu.CompilerParams(dimension_semantics=semantics,
                                vmem_limit_bytes=VMEM_LIMIT_BYTES)


def _dot(a, b):
    return jnp.dot(a, b, preferred_element_type=F32)


def _rmsnorm_kernel(x_ref, g_ref, o_ref):
    x = x_ref[...]
    r = lax.rsqrt(jnp.mean(x * x, axis=-1, keepdims=True) + EPS)
    o_ref[...] = (x * r * g_ref[...]).astype(o_ref.dtype)


def _rmsnorm(x, g, out_dtype, tm=256):
    t, d = x.shape
    return pl.pallas_call(
        _rmsnorm_kernel,
        grid=(t // tm,),
        in_specs=[pl.BlockSpec((tm, d), lambda i: (i, 0)),
                  pl.BlockSpec((1, d), lambda i: (0, 0))],
        out_specs=pl.BlockSpec((tm, d), lambda i: (i, 0)),
        out_shape=jax.ShapeDtypeStruct((t, d), out_dtype),
        compiler_params=_params("parallel"),
        name="rmsnorm",
    )(x, g.reshape(1, d))


def _mm_gelu_kernel(a_ref, b_ref, o_ref):
    o_ref[...] = jax.nn.gelu(_dot(a_ref[...], b_ref[...])).astype(o_ref.dtype)


def _mm_gelu(a, b, col_block_offset, n, out_dtype, tm=1024, tn=512):
    t, k = a.shape
    return pl.pallas_call(
        _mm_gelu_kernel,
        grid=(t // tm, n // tn),
        in_specs=[pl.BlockSpec((tm, k), lambda i, j: (i, 0)),
                  pl.BlockSpec((k, tn), lambda i, j: (0, j + col_block_offset))],
        out_specs=pl.BlockSpec((tm, tn), lambda i, j: (i, j)),
        out_shape=jax.ShapeDtypeStruct((t, n), out_dtype),
        compiler_params=_params("parallel", "arbitrary"),
        name="mm_gelu",
    )(a, b)


def _mm_residual_kernel(a_ref, b_ref, res_ref, o_ref):
    acc = _dot(a_ref[...], b_ref[...])

    @pl.when(pl.program_id(2) == 0)
    def _():
        o_ref[...] = res_ref[...] + acc

    @pl.when(pl.program_id(2) > 0)
    def _():
        o_ref[...] += acc


def _mm_residual(a, b, res, tm=1024, tn=512, tk=None):
    t, k = a.shape
    n = b.shape[1]
    tk = k if tk is None else tk
    return pl.pallas_call(
        _mm_residual_kernel,
        grid=(t // tm, n // tn, k // tk),
        in_specs=[pl.BlockSpec((tm, tk), lambda i, j, kk: (i, kk)),
                  pl.BlockSpec((tk, tn), lambda i, j, kk: (kk, j)),
                  pl.BlockSpec((tm, tn), lambda i, j, kk: (i, j))],
        out_specs=pl.BlockSpec((tm, tn), lambda i, j, kk: (i, j)),
        out_shape=jax.ShapeDtypeStruct((t, n), F32),
        compiler_params=_params("parallel", "arbitrary", "arbitrary"),
        name="mm_residual",
    )(a, b, res)


def _mm_swiglu_kernel(a_ref, wg_ref, wu_ref, o_ref):
    a = a_ref[...]
    g = _dot(a, wg_ref[...])
    u = _dot(a, wu_ref[...])
    o_ref[...] = (g * jax.nn.sigmoid(g) * u).astype(o_ref.dtype)


def _mm_swiglu(a, wg, wu, tm=1024, tn=256):
    t, k = a.shape
    n = wg.shape[1]
    return pl.pallas_call(
        _mm_swiglu_kernel,
        grid=(t // tm, n // tn),
        in_specs=[pl.BlockSpec((tm, k), lambda i, j: (i, 0)),
                  pl.BlockSpec((k, tn), lambda i, j: (0, j)),
                  pl.BlockSpec((k, tn), lambda i, j: (0, j))],
        out_specs=pl.BlockSpec((tm, tn), lambda i, j: (i, j)),
        out_shape=jax.ShapeDtypeStruct((t, n), BF16),
        compiler_params=_params("parallel", "arbitrary"),
        name="mm_swiglu",
    )(a, wg, wu)


def _sgu_kernel(u_ref, v_ref, g_ref, b_ref, ws_ref, bias_ref, o_ref, vn_ref, *,
                chunk, heads):
    v = v_ref[...]
    mu = jnp.mean(v, axis=-1, keepdims=True)
    vc = v - mu
    r = lax.rsqrt(jnp.mean(vc * vc, axis=-1, keepdims=True) + EPS)
    vn_ref[...] = (vc * r * g_ref[...] + b_ref[...]).astype(vn_ref.dtype)
    hd = v.shape[1] // heads
    for c in range(v.shape[0] // chunk):
        rows = pl.ds(c * chunk, chunk)
        for h in range(heads):
            cols = pl.ds(h * hd, hd)
            s = _dot(ws_ref[h], vn_ref[rows, cols]) + bias_ref[:, cols]
            o_ref[rows, cols] = (u_ref[rows, cols].astype(F32) * s).astype(o_ref.dtype)


def _sgu(u, v, ln_g, ln_b, w_s, b_s, tm=256):
    t, d = v.shape
    heads, chunk, _ = w_s.shape
    bias = jnp.repeat(jnp.transpose(b_s), d // heads, axis=1)
    row = lambda i: (i, 0)
    fixed2 = lambda i: (0, 0)
    return pl.pallas_call(
        functools.partial(_sgu_kernel, chunk=chunk, heads=heads),
        grid=(t // tm,),
        in_specs=[pl.BlockSpec((tm, d), row),
                  pl.BlockSpec((tm, d), row),
                  pl.BlockSpec((1, d), fixed2),
                  pl.BlockSpec((1, d), fixed2),
                  pl.BlockSpec((heads, chunk, chunk), lambda i: (0, 0, 0)),
                  pl.BlockSpec((chunk, d), fixed2)],
        out_specs=pl.BlockSpec((tm, d), row),
        out_shape=jax.ShapeDtypeStruct((t, d), BF16),
        scratch_shapes=[pltpu.VMEM((tm, d), BF16)],
        compiler_params=_params("parallel"),
        name="sgu",
    )(u, v, ln_g.reshape(1, d), ln_b.reshape(1, d), w_s.astype(BF16), bias)


def _dft_tables(n):
    idx = jnp.arange(n, dtype=jnp.int32)
    ang = ((idx[:, None] * idx[None, :]) % n).astype(F32) * (2.0 * math.pi / n)
    scale = 1.0 / math.sqrt(n)
    return (jnp.cos(ang) * scale).astype(BF16), (jnp.sin(ang) * scale).astype(BF16)


def _dft_channels_kernel(h_ref, cs_ref, p_ref, q_ref):
    acc = _dot(h_ref[...], cs_ref[...])
    gd = p_ref.shape[1]
    p_ref[...] = acc[:, :gd].astype(p_ref.dtype)
    q_ref[...] = acc[:, gd:].astype(q_ref.dtype)


def _dft_channels(h, cs, tm=1024):
    t, d = h.shape
    gd = d // B_GROUPS
    blk = pl.BlockSpec((tm, gd), lambda i, g: (i, g))
    return pl.pallas_call(
        _dft_channels_kernel,
        grid=(t // tm, B_GROUPS),
        in_specs=[blk, pl.BlockSpec((gd, 2 * gd), lambda i, g: (0, 0))],
        out_specs=[blk, blk],
        out_shape=[jax.ShapeDtypeStruct((t, d), BF16)] * 2,
        compiler_params=_params("parallel", "arbitrary"),
        name="dft_channels",
    )(h, cs)


def _dft_positions_kernel(c_ref, s_ref, p_ref, q_ref, o_ref):
    acc = _dot(c_ref[...], p_ref[...]) - _dot(s_ref[...], q_ref[...])
    o_ref[...] = acc.astype(o_ref.dtype)


def _dft_positions(cos_s, sin_s, p, q, tm=512, tn=512):
    b, s, d = p.shape
    trig = pl.BlockSpec((tm, s), lambda bb, i, j: (i, 0))
    data = pl.BlockSpec((None, s, tn), lambda bb, i, j: (bb, 0, j))
    return pl.pallas_call(
        _dft_positions_kernel,
        grid=(b, s // tm, d // tn),
        in_specs=[trig, trig, data, data],
        out_specs=pl.BlockSpec((None, tm, tn), lambda bb, i, j: (bb, i, j)),
        out_shape=jax.ShapeDtypeStruct((b, s, d), BF16),
        compiler_params=_params("parallel", "parallel", "arbitrary"),
        name="dft_positions",
    )(cos_s, sin_s, p, q)


def _ffn(x, norm_g, w_gate, w_up, w_down):
    h = _rmsnorm(x, norm_g, BF16)
    hidden = _mm_swiglu(h, w_gate, w_up)
    return _mm_residual(hidden, w_down, x, tk=w_down.shape[0] // 2)


def _gmlp_layer(x, norm_g, w_in, ln_g, ln_b, w_s, b_s, w_out):
    d_u = w_out.shape[0]
    h = _rmsnorm(x, norm_g, BF16)
    tn = 512
    u = _mm_gelu(h, w_in, 0, d_u, BF16, tn=tn)
    v = _mm_gelu(h, w_in, d_u // tn, d_u, F32, tn=tn)
    gated = _sgu(u, v, ln_g, ln_b, w_s, b_s)
    return _mm_residual(gated, w_out, x)


def _fourier_layer(x, batch, norm_g, w_out):
    t, d = x.shape
    seq = t // batch
    gd = d // B_GROUPS
    h = _rmsnorm(x, norm_g, BF16)
    cos_c, sin_c = _dft_tables(gd)
    cos_s, sin_s = _dft_tables(seq)
    p, q = _dft_channels(h, jnp.concatenate([cos_c, sin_c], axis=1))
    y = _dft_positions(cos_s, sin_s, p.reshape(batch, seq, d), q.reshape(batch, seq, d))
    return _mm_residual(y.reshape(t, d), w_out, x)


def _trunk(x3, w):
    batch, seq, d = x3.shape
    x = x3.reshape(batch * seq, d)
    depth = w["ffn_w_gate"].shape[0]
    for i in range(depth):
        j = i // 2
        if i % 2 == 0:
            x = _gmlp_layer(x, w["a_norm_g"][j], w["a_w_in"][j], w["a_ln_g"][j], w["a_ln_b"][j],
                            w["a_w_s"][j], w["a_b_s"][j], w["a_w_out"][j])
        else:
            x = _fourier_layer(x, batch, w["b_norm_g"][j], w["b_w_out"][j])
        x = _ffn(x, w["ffn_norm_g"][i], w["ffn_w_gate"][i], w["ffn_w_up"][i], w["ffn_w_down"][i])
    return _rmsnorm(x, w["final_norm_g"], F32).reshape(batch, seq, d)


def kernel(x_prompt, x_sample, a_norm_g, a_w_in, a_ln_g, a_ln_b, a_w_s, a_b_s, a_w_out,
           b_norm_g, b_w_out, ffn_norm_g, ffn_w_gate, ffn_w_up, ffn_w_down, final_norm_g):
    w = dict(a_norm_g=a_norm_g, a_w_in=a_w_in.astype(BF16), a_ln_g=a_ln_g, a_ln_b=a_ln_b,
             a_w_s=a_w_s, a_b_s=a_b_s, a_w_out=a_w_out.astype(BF16),
             b_norm_g=b_norm_g, b_w_out=b_w_out.astype(BF16), ffn_norm_g=ffn_norm_g,
             ffn_w_gate=ffn_w_gate.astype(BF16), ffn_w_up=ffn_w_up.astype(BF16),
             ffn_w_down=ffn_w_down.astype(BF16), final_norm_g=final_norm_g)
    return (_trunk(x_prompt, w), _trunk(x_sample, w))
```

```python
import functools
import math

import jax
import jax.numpy as jnp
from jax import lax
from jax.experimental import pallas as pl
from jax.experimental.pallas import tpu as pltpu

EPS = 1e-6
B_GROUPS = 8

V7X_VMEM_BYTES = 64 * 1024 * 1024
VMEM_LIMIT_BYTES = V7X_VMEM_BYTES - 8 * 1024 * 1024
LANES = 128
V7X_MXU_DIM = 256

F32 = jnp.float32
BF16 = jnp.bfloat16


def _params(*semantics, flags=None):
    return pltpu.CompilerParams(dimension_semantics=semantics,
                                vmem_limit_bytes=VMEM_LIMIT_BYTES, flags=flags)


def _dot(a, b):
    return jnp.dot(a, b, preferred_element_type=F32)


def _tile(total, preferred):
    t = min(total, preferred)
    assert total % t == 0, (total, preferred)
    return t


def _slices(total, width):
    width = min(width, total)
    assert total % width == 0, (total, width)
    return [slice(c * width, (c + 1) * width) for c in range(total // width)]


def _chunks(shape, rows=2 * V7X_MXU_DIM, cols=V7X_MXU_DIM, tail_rows=None):
    col_slices = _slices(shape[1], cols)
    out = [(r, c) for c in col_slices[:-1] for r in _slices(shape[0], rows)]
    return out + [(r, col_slices[-1]) for r in _slices(shape[0], tail_rows or rows)]


def _scale_rows(acc, r):
    return acc * jnp.tile(r, (1, acc.shape[1] // LANES))


def _row_rsqrt(ssq_lanes, d_model):
    tot = jnp.sum(ssq_lanes, axis=1, keepdims=True)
    return jnp.broadcast_to(lax.rsqrt(tot / d_model + EPS), ssq_lanes.shape)


def _lane_partial_ssq(x):
    sq = x * x
    part = sq[:, :LANES]
    for c in range(1, x.shape[1] // LANES):
        part = part + sq[:, c * LANES:(c + 1) * LANES]
    return part


def _prep_scaled_kernel(w_ref, g_ref, o_ref):
    o_ref[...] = (w_ref[...] * g_ref[...]).astype(o_ref.dtype)


def _prep_plain_kernel(w_ref, o_ref):
    o_ref[...] = w_ref[...].astype(o_ref.dtype)


def _prep_weight(w3, layer, g=None, tk=256):
    _, k, n = w3.shape
    tn = n
    while tn > 5504:
        tn //= 2
    assert n % tn == 0 and tn % LANES == 0 and k % tk == 0
    w_spec = pl.BlockSpec((None, tk, tn), lambda i, j: (layer, i, j))
    o_spec = pl.BlockSpec((tk, tn), lambda i, j: (i, j))
    common = dict(grid=(k // tk, n // tn), out_specs=o_spec,
                  out_shape=jax.ShapeDtypeStruct((k, n), BF16),
                  compiler_params=_params("parallel", "parallel"))
    if g is None:
        return pl.pallas_call(_prep_plain_kernel, in_specs=[w_spec], name="prep_plain", **common)(w3)
    g_spec = pl.BlockSpec((tk, 1), lambda i, j: (i, 0))
    return pl.pallas_call(_prep_scaled_kernel, in_specs=[w_spec, g_spec], name="prep_scaled",
                          **common)(w3, g.reshape(k, 1))


def _cast_ssq_kernel(x_ref, xb_ref, ssq_ref):
    x = x_ref[...]
    xb_ref[...] = x.astype(xb_ref.dtype)
    ssq_ref[...] = _lane_partial_ssq(x)


def _cast_ssq(x, tm=256):
    t, d = x.shape
    row = lambda i: (i, 0)
    return pl.pallas_call(
        _cast_ssq_kernel,
        grid=(t // tm,),
        in_specs=[pl.BlockSpec((tm, d), row)],
        out_specs=[pl.BlockSpec((tm, d), row), pl.BlockSpec((tm, LANES), row)],
        out_shape=[jax.ShapeDtypeStruct((t, d), BF16), jax.ShapeDtypeStruct((t, LANES), F32)],
        compiler_params=_params("parallel"),
        name="cast_ssq",
    )(x)


def _final_norm_kernel(x_ref, ssq_ref, g_ref, o_ref):
    x = x_ref[...]
    o_ref[...] = _scale_rows(x, _row_rsqrt(ssq_ref[...], x.shape[1])) * g_ref[...]


def _final_norm(x, ssq, g, tm=256):
    t, d = x.shape
    row = lambda i: (i, 0)
    return pl.pallas_call(
        _final_norm_kernel,
        grid=(t // tm,),
        in_specs=[pl.BlockSpec((tm, d), row), pl.BlockSpec((tm, LANES), row),
                  pl.BlockSpec((1, d), lambda i: (0, 0))],
        out_specs=pl.BlockSpec((tm, d), row),
        out_shape=jax.ShapeDtypeStruct((t, d), F32),
        compiler_params=_params("parallel"),
        name="final_norm",
    )(x, ssq, g.reshape(1, d))


def _row_scale_kernel(ssq_ref, r_ref, *, d_model):
    r_ref[...] = _row_rsqrt(ssq_ref[...], d_model)


def _row_scale(ssq, d_model, tm=2048):
    t = ssq.shape[0]
    tm = _tile(t, tm)
    spec = pl.BlockSpec((tm, LANES), lambda i: (i, 0))
    return pl.pallas_call(
        functools.partial(_row_scale_kernel, d_model=d_model),
        grid=(t // tm,), in_specs=[spec], out_specs=spec,
        out_shape=jax.ShapeDtypeStruct(ssq.shape, F32),
        compiler_params=_params("parallel"),
        name="row_scale",
    )(ssq)


def _mm_gelu_kernel(a_ref, r_ref, b_ref, o_ref):
    r = r_ref[...]
    for cols in _slices(o_ref.shape[1], V7X_MXU_DIM):
        acc = _scale_rows(_dot(a_ref[...], b_ref[:, cols]), r)
        o_ref[:, cols] = jax.nn.gelu(acc).astype(o_ref.dtype)


def _mm_gelu(a, r, b, col_block_offset, n, out_dtype, tm=1024, tn=1024):
    t, k = a.shape
    tm, tn = _tile(t, tm), _tile(n, tn)
    return pl.pallas_call(
        _mm_gelu_kernel,
        grid=(t // tm, n // tn),
        in_specs=[pl.BlockSpec((tm, k), lambda i, j: (i, 0)),
                  pl.BlockSpec((tm, LANES), lambda i, j: (i, 0)),
                  pl.BlockSpec((k, tn), lambda i, j: (0, j + col_block_offset))],
        out_specs=pl.BlockSpec((tm, tn), lambda i, j: (i, j)),
        out_shape=jax.ShapeDtypeStruct((t, n), out_dtype),
        compiler_params=_params("parallel", "arbitrary"),
        name="mm_gelu",
    )(a, r, b)


def _mm_swiglu_kernel(a_ref, ssq_ref, wg_ref, wu_ref, o_ref, r_ref):
    r_ref[...] = _row_rsqrt(ssq_ref[...], a_ref.shape[1])
    for rows, cols in _chunks(o_ref.shape):
        a = a_ref[rows, :]
        r = r_ref[rows, :]
        g = _scale_rows(_dot(a, wg_ref[:, cols]), r)
        u = _scale_rows(_dot(a, wu_ref[:, cols]), r)
        o_ref[rows, cols] = (g * jax.nn.sigmoid(g) * u).astype(o_ref.dtype)


def _mm_swiglu(a, ssq, wg, wu, tm=2048, tn=V7X_MXU_DIM):
    t, k = a.shape
    n = wg.shape[1]
    tm, tn = _tile(t, tm), _tile(n, tn)
    w_spec = pl.BlockSpec((k, tn), lambda i, j: (0, j))
    return pl.pallas_call(
        _mm_swiglu_kernel,
        grid=(t // tm, n // tn),
        in_specs=[pl.BlockSpec((tm, k), lambda i, j: (i, 0)),
                  pl.BlockSpec((tm, LANES), lambda i, j: (i, 0)),
                  w_spec, w_spec],
        out_specs=pl.BlockSpec((tm, tn), lambda i, j: (i, j)),
        out_shape=jax.ShapeDtypeStruct((t, n), BF16),
        scratch_shapes=[pltpu.VMEM((tm, LANES), F32)],
        compiler_params=_params("parallel", "arbitrary"),
        name="mm_swiglu",
    )(a, ssq, wg, wu)


def _mm_residual_kernel(a_ref, b_ref, res_ref, o_ref, *norm_refs):
    if norm_refs:
        xb_ref, ssq_ref = norm_refs

        @pl.when(pl.program_id(1) == 0)
        def _():
            ssq_ref[...] = jnp.zeros_like(ssq_ref)

    for rows, cols in _chunks(o_ref.shape):
        x_new = res_ref[rows, cols] + _dot(a_ref[rows, :], b_ref[:, cols])
        o_ref[rows, cols] = x_new
        if norm_refs:
            xb_ref[rows, cols] = x_new.astype(xb_ref.dtype)
            ssq_ref[rows, :] += _lane_partial_ssq(x_new)


def _mm_residual(a, b, res, k_block=0, k_blocks=1, emit_norm=False, tm=1024, tn=512):
    t, k = a.shape
    n = b.shape[1]
    tk = k // k_blocks
    tm, tn = _tile(t, tm), _tile(n, tn)
    tile = pl.BlockSpec((tm, tn), lambda i, j: (i, j))
    out_specs = [tile]
    out_shape = [jax.ShapeDtypeStruct((t, n), F32)]
    if emit_norm:
        out_specs += [tile, pl.BlockSpec((tm, LANES), lambda i, j: (i, 0))]
        out_shape += [jax.ShapeDtypeStruct((t, n), BF16), jax.ShapeDtypeStruct((t, LANES), F32)]
    out = pl.pallas_call(
        _mm_residual_kernel,
        grid=(t // tm, n // tn),
        in_specs=[pl.BlockSpec((tm, tk), lambda i, j: (i, k_block)),
                  pl.BlockSpec((tk, tn), lambda i, j: (k_block, j)),
                  tile],
        out_specs=out_specs,
        out_shape=out_shape,
        compiler_params=_params("parallel", "arbitrary"),
        name="mm_residual_norm" if emit_norm else "mm_residual",
    )(a, b, res)
    return out if emit_norm else out[0]


def _sgu_kernel(u_ref, v_ref, g_ref, b_ref, ws_ref, bias_ref, o_ref, vn_ref, *,
                chunk, heads):
    v = v_ref[...]
    mu = jnp.mean(v, axis=-1, keepdims=True)
    vc = v - mu
    r = lax.rsqrt(jnp.mean(vc * vc, axis=-1, keepdims=True) + EPS)
    vn_ref[...] = (vc * r * g_ref[...] + b_ref[...]).astype(vn_ref.dtype)
    hd = v.shape[1] // heads
    for c in range(v.shape[0] // chunk):
        rows = pl.ds(c * chunk, chunk)
        for h in range(heads):
            cols = pl.ds(h * hd, hd)
            s = _dot(ws_ref[h], vn_ref[rows, cols]) + bias_ref[:, cols]
            o_ref[rows, cols] = (u_ref[rows, cols].astype(F32) * s).astype(o_ref.dtype)


def _sgu(u, v, ln_g, ln_b, w_s, b_s, tm=256):
    t, d = v.shape
    heads, chunk, _ = w_s.shape
    bias = jnp.repeat(jnp.transpose(b_s), d // heads, axis=1)
    row = lambda i: (i, 0)
    fixed2 = lambda i: (0, 0)
    return pl.pallas_call(
        functools.partial(_sgu_kernel, chunk=chunk, heads=heads),
        grid=(t // tm,),
        in_specs=[pl.BlockSpec((tm, d), row),
                  pl.BlockSpec((tm, d), row),
                  pl.BlockSpec((1, d), fixed2),
                  pl.BlockSpec((1, d), fixed2),
                  pl.BlockSpec((heads, chunk, chunk), lambda i: (0, 0, 0)),
                  pl.BlockSpec((chunk, d), fixed2)],
        out_specs=pl.BlockSpec((tm, d), row),
        out_shape=jax.ShapeDtypeStruct((t, d), BF16),
        scratch_shapes=[pltpu.VMEM((tm, d), BF16)],
        compiler_params=_params("parallel"),
        name="sgu",
    )(u, v, ln_g.reshape(1, d), ln_b.reshape(1, d), w_s.astype(BF16), bias)


def _dft_tables(n, split=64):
    split = min(split, n)
    j = jnp.arange(n, dtype=jnp.int32)[None, :]
    unit = 2.0 * math.pi / n
    a_lo = ((jnp.arange(split, dtype=jnp.int32)[:, None] * j) % n).astype(F32) * unit
    a_hi = (((jnp.arange(n // split, dtype=jnp.int32) * split)[:, None] * j) % n).astype(F32) * unit
    scale = 1.0 / math.sqrt(n)
    c_lo, s_lo = jnp.cos(a_lo)[None, :, :], jnp.sin(a_lo)[None, :, :]
    c_hi, s_hi = (jnp.cos(a_hi) * scale)[:, None, :], (jnp.sin(a_hi) * scale)[:, None, :]
    cos = (c_hi * c_lo - s_hi * s_lo).reshape(n, n)
    sin = (s_hi * c_lo + c_hi * s_lo).reshape(n, n)
    return cos, sin


def _dft_channels_kernel(h_ref, ssq_ref, cs_ref, pq_ref):
    r = _row_rsqrt(ssq_ref[...], h_ref.shape[1])
    gd = cs_ref.shape[1]
    for g in range(cs_ref.shape[0]):
        cols = slice(g * gd, (g + 1) * gd)
        for part in range(2):
            acc = _dot(h_ref[:, cols], cs_ref[g, :, part * gd:(part + 1) * gd])
            pq_ref[part, :, cols] = _scale_rows(acc, r).astype(pq_ref.dtype)


def _dft_channels(h, ssq, cs, batch, tm=512):
    t, d = h.shape
    seq = t // batch
    tm = _tile(seq, tm)
    per_seq = seq // tm
    row = lambda i: (i, 0)
    return pl.pallas_call(
        _dft_channels_kernel,
        grid=(t // tm,),
        in_specs=[pl.BlockSpec((tm, d), row), pl.BlockSpec((tm, LANES), row),
                  pl.BlockSpec(cs.shape, lambda i: (0, 0, 0))],
        out_specs=pl.BlockSpec((None, 2, tm, d), lambda i: (i // per_seq, 0, i % per_seq, 0)),
        out_shape=jax.ShapeDtypeStruct((batch, 2, seq, d), BF16),
        compiler_params=_params("parallel"),
        name="dft_channels",
    )(h, ssq, cs)


def _dft_positions_kernel(trig_ref, pq_ref, o_ref):
    for rows, cols in _chunks(o_ref.shape):
        o_ref[rows, cols] = _dot(trig_ref[rows, :], pq_ref[:, cols]).astype(o_ref.dtype)


def _dft_positions(trig, pq):
    b, k, d = pq.shape
    s = k // 2
    tm, tn = (1024, 1024) if s <= 2048 else (512, 512)
    tm, tn = _tile(s, tm), _tile(d, tn)
    return pl.pallas_call(
        _dft_positions_kernel,
        grid=(b, s // tm, d // tn),
        in_specs=[pl.BlockSpec((tm, k), lambda bb, i, j: (i, 0)),
                  pl.BlockSpec((None, k, tn), lambda bb, i, j: (bb, 0, j))],
        out_specs=pl.BlockSpec((None, tm, tn), lambda bb, i, j: (bb, i, j)),
        out_shape=jax.ShapeDtypeStruct((b, s, d), BF16),
        compiler_params=_params("parallel", "parallel", "arbitrary"),
        name="dft_positions",
    )(trig, pq)


def _ffn(x, xb, ssq, w_gate, w_up, w_down):
    hidden = _mm_swiglu(xb, ssq, w_gate, w_up)
    x_mid = _mm_residual(hidden, w_down, x, k_block=0, k_blocks=2)
    return _mm_residual(hidden, w_down, x_mid, k_block=1, k_blocks=2, emit_norm=True)


def _gmlp_layer(x, xb, ssq, w_in, ln_g, ln_b, w_s, b_s, w_out):
    d_u = w_out.shape[0]
    tn = _tile(d_u, 1024)
    r = _row_scale(ssq, x.shape[1])
    u = _mm_gelu(xb, r, w_in, 0, d_u, BF16, tn=tn)
    v = _mm_gelu(xb, r, w_in, d_u // tn, d_u, F32, tn=tn)
    gated = _sgu(u, v, ln_g, ln_b, w_s, b_s)
    return _mm_residual(gated, w_out, x, emit_norm=True)


def _fourier_layer(x, xb, ssq, batch, cs, w_out):
    t, d = x.shape
    seq = t // batch
    cos_s, sin_s = _dft_tables(seq)
    trig = jnp.concatenate([cos_s, -sin_s], axis=1).astype(BF16)
    pq = _dft_channels(xb, ssq, cs, batch)
    y = _dft_positions(trig, pq.reshape(batch, 2 * seq, d))
    return _mm_residual(y.reshape(t, d), w_out, x, emit_norm=True)


def _trunk(x3, layers, final_norm_g):
    batch, seq, d = x3.shape
    x = x3.reshape(batch * seq, d)
    xb, ssq = _cast_ssq(x)
    for i, layer in enumerate(layers):
        if i % 2 == 0:
            x, xb, ssq = _gmlp_layer(x, xb, ssq, *layer["mixer"])
        else:
            x, xb, ssq = _fourier_layer(x, xb, ssq, batch, *layer["mixer"])
        x, xb, ssq = _ffn(x, xb, ssq, *layer["ffn"])
    return _final_norm(x, ssq, final_norm_g).reshape(batch, seq, d)


def kernel(x_prompt, x_sample, a_norm_g, a_w_in, a_ln_g, a_ln_b, a_w_s, a_b_s, a_w_out,
           b_norm_g, b_w_out, ffn_norm_g, ffn_w_gate, ffn_w_up, ffn_w_down, final_norm_g):
    depth, d, _ = ffn_w_gate.shape
    gd = d // B_GROUPS
    cos_c, sin_c = _dft_tables(gd)
    cs3 = jnp.tile(jnp.concatenate([cos_c, sin_c], axis=1), (B_GROUPS, 1))[None]
    layers = []
    for i in range(depth):
        j = i // 2
        if i % 2 == 0:
            mixer = (_prep_weight(a_w_in, j, a_norm_g[j]), a_ln_g[j], a_ln_b[j], a_w_s[j], a_b_s[j],
                     _prep_weight(a_w_out, j))
        else:
            cs = _prep_weight(cs3, 0, b_norm_g[j]).reshape(B_GROUPS, gd, 2 * gd)
            mixer = (cs, _prep_weight(b_w_out, j))
        ffn = (_prep_weight(ffn_w_gate, i, ffn_norm_g[i]), _prep_weight(ffn_w_up, i, ffn_norm_g[i]),
               _prep_weight(ffn_w_down, i))
        layers.append(dict(mixer=mixer, ffn=ffn))
    return (_trunk(x_prompt, layers, final_norm_g), _trunk(x_sample, layers, final_norm_g))
```

```python
import functools
import math

import jax
import jax.numpy as jnp
from jax import lax
from jax.experimental import pallas as pl
from jax.experimental.pallas import tpu as pltpu

EPS = 1e-6
B_GROUPS = 8
DFT_RADIX = 4

V7X_VMEM_BYTES = 64 * 1024 * 1024
VMEM_LIMIT_BYTES = V7X_VMEM_BYTES - 8 * 1024 * 1024
LANES = 128
BF16_ROWS = 16
V7X_MXU_DIM = 256

F32 = jnp.float32
BF16 = jnp.bfloat16


def _params(*semantics, flags=None):
    return pltpu.CompilerParams(dimension_semantics=semantics,
                                vmem_limit_bytes=VMEM_LIMIT_BYTES, flags=flags)


def _dot(a, b):
    return jnp.dot(a, b, preferred_element_type=F32)


def _tile(total, preferred):
    t = min(total, preferred)
    assert total % t == 0, (total, preferred)
    return t


def _slices(total, width):
    width = min(width, total)
    assert total % width == 0, (total, width)
    return [slice(c * width, (c + 1) * width) for c in range(total // width)]


def _chunks(shape, rows=2 * V7X_MXU_DIM, cols=V7X_MXU_DIM, tail_rows=None):
    col_slices = _slices(shape[1], cols)
    out = [(r, c) for c in col_slices[:-1] for r in _slices(shape[0], rows)]
    return out + [(r, col_slices[-1]) for r in _slices(shape[0], tail_rows or rows)]


def _scale_rows(acc, r):
    return acc * jnp.tile(r, (1, acc.shape[1] // LANES))


def _row_rsqrt(ssq_lanes, d_model):
    tot = jnp.sum(ssq_lanes, axis=1, keepdims=True)
    return jnp.broadcast_to(lax.rsqrt(tot / d_model + EPS), ssq_lanes.shape)


def _lane_partial_ssq(x):
    sq = x * x
    part = sq[:, :LANES]
    for c in range(1, x.shape[1] // LANES):
        part = part + sq[:, c * LANES:(c + 1) * LANES]
    return part


def _prep_scaled_kernel(w_ref, g_ref, o_ref):
    o_ref[...] = (w_ref[...] * g_ref[...]).astype(o_ref.dtype)


def _prep_plain_kernel(w_ref, o_ref):
    o_ref[...] = w_ref[...].astype(o_ref.dtype)


def _prep_weight(w3, layer, g=None, tk=256):
    _, k, n = w3.shape
    tn = n
    while tn > 5504:
        tn //= 2
    assert n % tn == 0 and tn % LANES == 0 and k % tk == 0
    w_spec = pl.BlockSpec((None, tk, tn), lambda i, j: (layer, i, j))
    o_spec = pl.BlockSpec((tk, tn), lambda i, j: (i, j))
    common = dict(grid=(k // tk, n // tn), out_specs=o_spec,
                  out_shape=jax.ShapeDtypeStruct((k, n), BF16),
                  compiler_params=_params("parallel", "parallel"))
    if g is None:
        return pl.pallas_call(_prep_plain_kernel, in_specs=[w_spec], name="prep_plain", **common)(w3)
    g_spec = pl.BlockSpec((tk, 1), lambda i, j: (i, 0))
    return pl.pallas_call(_prep_scaled_kernel, in_specs=[w_spec, g_spec], name="prep_scaled",
                          **common)(w3, g.reshape(k, 1))


def _cast_ssq_kernel(x_ref, xb_ref, ssq_ref):
    x = x_ref[...]
    xb_ref[...] = x.astype(xb_ref.dtype)
    ssq_ref[...] = _lane_partial_ssq(x)


def _cast_ssq(x, tm=256):
    t, d = x.shape
    row = lambda i: (i, 0)
    return pl.pallas_call(
        _cast_ssq_kernel,
        grid=(t // tm,),
        in_specs=[pl.BlockSpec((tm, d), row)],
        out_specs=[pl.BlockSpec((tm, d), row), pl.BlockSpec((tm, LANES), row)],
        out_shape=[jax.ShapeDtypeStruct((t, d), BF16), jax.ShapeDtypeStruct((t, LANES), F32)],
        compiler_params=_params("parallel"),
        name="cast_ssq",
    )(x)


def _final_norm_kernel(x_ref, ssq_ref, g_ref, o_ref):
    x = x_ref[...]
    o_ref[...] = _scale_rows(x, _row_rsqrt(ssq_ref[...], x.shape[1])) * g_ref[...]


def _final_norm(x, ssq, g, tm=256):
    t, d = x.shape
    row = lambda i: (i, 0)
    return pl.pallas_call(
        _final_norm_kernel,
        grid=(t // tm,),
        in_specs=[pl.BlockSpec((tm, d), row), pl.BlockSpec((tm, LANES), row),
                  pl.BlockSpec((1, d), lambda i: (0, 0))],
        out_specs=pl.BlockSpec((tm, d), row),
        out_shape=jax.ShapeDtypeStruct((t, d), F32),
        compiler_params=_params("parallel"),
        name="final_norm",
    )(x, ssq, g.reshape(1, d))


def _row_scale_kernel(ssq_ref, r_ref, *, d_model):
    r_ref[...] = _row_rsqrt(ssq_ref[...], d_model)


def _row_scale(ssq, d_model, tm=2048):
    t = ssq.shape[0]
    tm = _tile(t, tm)
    spec = pl.BlockSpec((tm, LANES), lambda i: (i, 0))
    return pl.pallas_call(
        functools.partial(_row_scale_kernel, d_model=d_model),
        grid=(t // tm,), in_specs=[spec], out_specs=spec,
        out_shape=jax.ShapeDtypeStruct(ssq.shape, F32),
        compiler_params=_params("parallel"),
        name="row_scale",
    )(ssq)


def _mm_gelu_kernel(a_ref, r_ref, b_ref, o_ref):
    r = r_ref[...]
    for cols in _slices(o_ref.shape[1], V7X_MXU_DIM):
        acc = _scale_rows(_dot(a_ref[...], b_ref[:, cols]), r)
        o_ref[:, cols] = jax.nn.gelu(acc).astype(o_ref.dtype)


def _mm_gelu(a, r, b, col_block_offset, n, out_dtype, tm=1024, tn=1024):
    t, k = a.shape
    tm, tn = _tile(t, tm), _tile(n, tn)
    return pl.pallas_call(
        _mm_gelu_kernel,
        grid=(t // tm, n // tn),
        in_specs=[pl.BlockSpec((tm, k), lambda i, j: (i, 0)),
                  pl.BlockSpec((tm, LANES), lambda i, j: (i, 0)),
                  pl.BlockSpec((k, tn), lambda i, j: (0, j + col_block_offset))],
        out_specs=pl.BlockSpec((tm, tn), lambda i, j: (i, j)),
        out_shape=jax.ShapeDtypeStruct((t, n), out_dtype),
        compiler_params=_params("parallel", "arbitrary"),
        name="mm_gelu",
    )(a, r, b)


def _mm_swiglu_kernel(a_ref, ssq_ref, wg_ref, wu_ref, o_ref, r_ref):
    r_ref[...] = _row_rsqrt(ssq_ref[...], a_ref.shape[1])
    for rows, cols in _chunks(o_ref.shape):
        a = a_ref[rows, :]
        r = r_ref[rows, :]
        g = _scale_rows(_dot(a, wg_ref[:, cols]), r)
        u = _scale_rows(_dot(a, wu_ref[:, cols]), r)
        o_ref[rows, cols] = (g * jax.nn.sigmoid(g) * u).astype(o_ref.dtype)


def _mm_swiglu(a, ssq, wg, wu, tm=2048, tn=V7X_MXU_DIM):
    t, k = a.shape
    n = wg.shape[1]
    tm, tn = _tile(t, tm), _tile(n, tn)
    w_spec = pl.BlockSpec((k, tn), lambda i, j: (0, j))
    return pl.pallas_call(
        _mm_swiglu_kernel,
        grid=(t // tm, n // tn),
        in_specs=[pl.BlockSpec((tm, k), lambda i, j: (i, 0)),
                  pl.BlockSpec((tm, LANES), lambda i, j: (i, 0)),
                  w_spec, w_spec],
        out_specs=pl.BlockSpec((tm, tn), lambda i, j: (i, j)),
        out_shape=jax.ShapeDtypeStruct((t, n), BF16),
        scratch_shapes=[pltpu.VMEM((tm, LANES), F32)],
        compiler_params=_params("parallel", "arbitrary"),
        name="mm_swiglu",
    )(a, ssq, wg, wu)


def _mm_residual_kernel(a_ref, b_ref, res_ref, o_ref, *norm_refs):
    if norm_refs:
        xb_ref, ssq_ref = norm_refs

        @pl.when(pl.program_id(1) == 0)
        def _():
            ssq_ref[...] = jnp.zeros_like(ssq_ref)

    for rows, cols in _chunks(o_ref.shape):
        x_new = res_ref[rows, cols] + _dot(a_ref[rows, :], b_ref[:, cols])
        o_ref[rows, cols] = x_new
        if norm_refs:
            xb_ref[rows, cols] = x_new.astype(xb_ref.dtype)
            ssq_ref[rows, :] += _lane_partial_ssq(x_new)


def _mm_residual(a, b, res, k_block=0, k_blocks=1, emit_norm=False, tm=1024, tn=512):
    t, k = a.shape
    n = b.shape[1]
    tk = k // k_blocks
    tm, tn = _tile(t, tm), _tile(n, tn)
    tile = pl.BlockSpec((tm, tn), lambda i, j: (i, j))
    out_specs = [tile]
    out_shape = [jax.ShapeDtypeStruct((t, n), F32)]
    if emit_norm:
        out_specs += [tile, pl.BlockSpec((tm, LANES), lambda i, j: (i, 0))]
        out_shape += [jax.ShapeDtypeStruct((t, n), BF16), jax.ShapeDtypeStruct((t, LANES), F32)]
    out = pl.pallas_call(
        _mm_residual_kernel,
        grid=(t // tm, n // tn),
        in_specs=[pl.BlockSpec((tm, tk), lambda i, j: (i, k_block)),
                  pl.BlockSpec((tk, tn), lambda i, j: (k_block, j)),
                  tile],
        out_specs=out_specs,
        out_shape=out_shape,
        compiler_params=_params("parallel", "arbitrary"),
        name="mm_residual_norm" if emit_norm else "mm_residual",
    )(a, b, res)
    return out if emit_norm else out[0]


def _sgu_kernel(u_ref, v_ref, g_ref, b_ref, ws_ref, bias_ref, o_ref, vn_ref, *,
                chunk, heads):
    v = v_ref[...]
    mu = jnp.mean(v, axis=-1, keepdims=True)
    vc = v - mu
    r = lax.rsqrt(jnp.mean(vc * vc, axis=-1, keepdims=True) + EPS)
    vn_ref[...] = (vc * r * g_ref[...] + b_ref[...]).astype(vn_ref.dtype)
    hd = v.shape[1] // heads
    for c in range(v.shape[0] // chunk):
        rows = pl.ds(c * chunk, chunk)
        for h in range(heads):
            cols = pl.ds(h * hd, hd)
            s = _dot(ws_ref[h], vn_ref[rows, cols]) + bias_ref[:, cols]
            o_ref[rows, cols] = (u_ref[rows, cols].astype(F32) * s).astype(o_ref.dtype)


def _sgu(u, v, ln_g, ln_b, w_s, b_s, tm=256):
    t, d = v.shape
    heads, chunk, _ = w_s.shape
    bias = jnp.repeat(jnp.transpose(b_s), d // heads, axis=1)
    row = lambda i: (i, 0)
    fixed2 = lambda i: (0, 0)
    return pl.pallas_call(
        functools.partial(_sgu_kernel, chunk=chunk, heads=heads),
        grid=(t // tm,),
        in_specs=[pl.BlockSpec((tm, d), row),
                  pl.BlockSpec((tm, d), row),
                  pl.BlockSpec((1, d), fixed2),
                  pl.BlockSpec((1, d), fixed2),
                  pl.BlockSpec((heads, chunk, chunk), lambda i: (0, 0, 0)),
                  pl.BlockSpec((chunk, d), fixed2)],
        out_specs=pl.BlockSpec((tm, d), row),
        out_shape=jax.ShapeDtypeStruct((t, d), BF16),
        scratch_shapes=[pltpu.VMEM((tm, d), BF16)],
        compiler_params=_params("parallel"),
        name="sgu",
    )(u, v, ln_g.reshape(1, d), ln_b.reshape(1, d), w_s.astype(BF16), bias)


def _dft_tables(n, cols=None, split=64):
    cols = n if cols is None else cols
    split = min(split, n)
    j = jnp.arange(cols, dtype=jnp.int32)[None, :]
    unit = 2.0 * math.pi / n
    a_lo = ((jnp.arange(split, dtype=jnp.int32)[:, None] * j) % n).astype(F32) * unit
    a_hi = (((jnp.arange(n // split, dtype=jnp.int32) * split)[:, None] * j) % n).astype(F32) * unit
    scale = 1.0 / math.sqrt(n)
    c_lo, s_lo = jnp.cos(a_lo)[None, :, :], jnp.sin(a_lo)[None, :, :]
    c_hi, s_hi = (jnp.cos(a_hi) * scale)[:, None, :], (jnp.sin(a_hi) * scale)[:, None, :]
    cos = (c_hi * c_lo - s_hi * s_lo).reshape(n, cols)
    sin = (s_hi * c_lo + c_hi * s_lo).reshape(n, cols)
    return cos, sin


def _position_tables(seq):
    quarter = seq // DFT_RADIX
    cos, sin = _dft_tables(seq, cols=quarter)
    trig = jnp.concatenate([cos, -sin], axis=1).astype(BF16)
    return jnp.transpose(trig.reshape(quarter, DFT_RADIX, 2 * quarter), (1, 0, 2))


def _dft_channels_kernel(h_ref, ssq_ref, cs_ref, pq_ref, r_ref):
    radix, tm, d = h_ref.shape
    groups, gd, _ = cs_ref.shape
    width = min(V7X_MXU_DIM, gd)
    for b in range(radix):
        r_ref[b] = _row_rsqrt(ssq_ref[b], d)
    granules = tm // 2 // BF16_ROWS
    for t0 in (0, tm // 2):
        for g in range(groups):
            gcols = slice(g * gd, (g + 1) * gd)
            lhs = jnp.concatenate(
                [h_ref[b, t0 + u * BF16_ROWS:t0 + (u + 1) * BF16_ROWS, gcols]
                 for u in range(granules) for b in range(radix)], axis=0)
            for si in range(gd // width):
                acc = _dot(lhs, cs_ref[g, :, 2 * si * width:2 * (si + 1) * width])
                cols = slice(g * gd + si * width, g * gd + (si + 1) * width)
                for u in range(granules):
                    rows = slice(t0 + u * BF16_ROWS, t0 + (u + 1) * BF16_ROWS)
                    p, q = [], []
                    for b in range(radix):
                        lo = (u * radix + b) * BF16_ROWS
                        blk = _scale_rows(acc[lo:lo + BF16_ROWS], r_ref[b, rows, :])
                        p.append(blk[:, :width])
                        q.append(blk[:, width:])
                    pa, pb, pc, pd = p[0] + p[2], p[1] + p[3], p[0] - p[2], p[1] - p[3]
                    qa, qb, qc, qd = q[0] + q[2], q[1] + q[3], q[0] - q[2], q[1] - q[3]
                    out = [(pa + pb, qa + qb), (pc - qd, qc + pd), (pa - pb, qa - qb), (pc + qd, qc - pd)]
                    for k, (vp, vq) in enumerate(out):
                        pq_ref[k, 0, rows, cols] = vp.astype(pq_ref.dtype)
                        pq_ref[k, 1, rows, cols] = vq.astype(pq_ref.dtype)


def _dft_channels(h, ssq, cs, batch, tm=128):
    t, d = h.shape
    seq = t // batch
    quarter = seq // DFT_RADIX
    tm = _tile(quarter, tm)
    return pl.pallas_call(
        _dft_channels_kernel,
        grid=(batch, quarter // tm),
        in_specs=[pl.BlockSpec((None, DFT_RADIX, tm, d), lambda b, i: (b, 0, i, 0)),
                  pl.BlockSpec((None, DFT_RADIX, tm, LANES), lambda b, i: (b, 0, i, 0)),
                  pl.BlockSpec(cs.shape, lambda b, i: (0, 0, 0))],
        out_specs=pl.BlockSpec((None, DFT_RADIX, 2, tm, d), lambda b, i: (b, 0, 0, i, 0)),
        out_shape=jax.ShapeDtypeStruct((batch, DFT_RADIX, 2, quarter, d), BF16),
        scratch_shapes=[pltpu.VMEM((DFT_RADIX, tm, LANES), F32)],
        compiler_params=_params("parallel", "parallel"),
        name="dft_channels",
    )(h.reshape(batch, DFT_RADIX, quarter, d), ssq.reshape(batch, DFT_RADIX, quarter, LANES), cs)


def _dft_positions_kernel(trig_ref, pq_ref, o_ref):
    for rows, cols in _chunks(o_ref.shape):
        o_ref[rows, cols] = _dot(trig_ref[rows, :], pq_ref[:, cols]).astype(o_ref.dtype)


def _dft_positions(trig, pq):
    batch, radix, k2, d = pq.shape
    quarter = k2 // 2
    tn = _tile(d, (8 * 1024 * 1024) // (2 * k2))
    return pl.pallas_call(
        _dft_positions_kernel,
        grid=(radix, batch, d // tn),
        in_specs=[pl.BlockSpec((None, quarter, k2), lambda k, b, j: (k, 0, 0)),
                  pl.BlockSpec((None, None, k2, tn), lambda k, b, j: (b, k, 0, j))],
        out_specs=pl.BlockSpec((None, quarter, tn), lambda k, b, j: (b, 0, k * (d // tn) + j)),
        out_shape=jax.ShapeDtypeStruct((batch, quarter, radix * d), BF16),
        compiler_params=_params("parallel", "parallel", "arbitrary"),
        name="dft_positions",
    )(trig, pq)


def _ffn(x, xb, ssq, w_gate, w_up, w_down):
    hidden = _mm_swiglu(xb, ssq, w_gate, w_up)
    x_mid = _mm_residual(hidden, w_down, x, k_block=0, k_blocks=2)
    return _mm_residual(hidden, w_down, x_mid, k_block=1, k_blocks=2, emit_norm=True)


def _gmlp_layer(x, xb, ssq, w_in, ln_g, ln_b, w_s, b_s, w_out):
    d_u = w_out.shape[0]
    tn = _tile(d_u, 1024)
    r = _row_scale(ssq, x.shape[1])
    u = _mm_gelu(xb, r, w_in, 0, d_u, BF16, tn=tn)
    v = _mm_gelu(xb, r, w_in, d_u // tn, d_u, F32, tn=tn)
    gated = _sgu(u, v, ln_g, ln_b, w_s, b_s)
    return _mm_residual(gated, w_out, x, emit_norm=True)


def _fourier_layer(x, xb, ssq, batch, cs, w_out):
    t, d = x.shape
    seq = t // batch
    quarter = seq // DFT_RADIX
    pq = _dft_channels(xb, ssq, cs, batch)
    y = _dft_positions(_position_tables(seq), pq.reshape(batch, DFT_RADIX, 2 * quarter, d))
    return _mm_residual(y.reshape(t, d), w_out, x, emit_norm=True)


def _trunk(x3, layers, final_norm_g):
    batch, seq, d = x3.shape
    x = x3.reshape(batch * seq, d)
    xb, ssq = _cast_ssq(x)
    for i, layer in enumerate(layers):
        if i % 2 == 0:
            x, xb, ssq = _gmlp_layer(x, xb, ssq, *layer["mixer"])
        else:
            x, xb, ssq = _fourier_layer(x, xb, ssq, batch, *layer["mixer"])
        x, xb, ssq = _ffn(x, xb, ssq, *layer["ffn"])
    return _final_norm(x, ssq, final_norm_g).reshape(batch, seq, d)


def kernel(x_prompt, x_sample, a_norm_g, a_w_in, a_ln_g, a_ln_b, a_w_s, a_b_s, a_w_out,
           b_norm_g, b_w_out, ffn_norm_g, ffn_w_gate, ffn_w_up, ffn_w_down, final_norm_g):
    depth, d, _ = ffn_w_gate.shape
    gd = d // B_GROUPS
    width = min(V7X_MXU_DIM, gd)
    cs_pair = jnp.stack([t.reshape(gd, gd // width, width) for t in _dft_tables(gd)], axis=2)
    cs3 = jnp.tile(cs_pair.reshape(gd, 2 * gd), (B_GROUPS, 1))[None]
    layers = []
    for i in range(depth):
        j = i // 2
        if i % 2 == 0:
            mixer = (_prep_weight(a_w_in, j, a_norm_g[j]), a_ln_g[j], a_ln_b[j], a_w_s[j], a_b_s[j],
                     _prep_weight(a_w_out, j))
        else:
            cs = _prep_weight(cs3, 0, b_norm_g[j]).reshape(B_GROUPS, gd, 2 * gd)
            mixer = (cs, _prep_weight(b_w_out, j))
        ffn = (_prep_weight(ffn_w_gate, i, ffn_norm_g[i]), _prep_weight(ffn_w_up, i, ffn_norm_g[i]),
               _prep_weight(ffn_w_down, i))
        layers.append(dict(mixer=mixer, ffn=ffn))
    return (_trunk(x_prompt, layers, final_norm_g), _trunk(x_sample, layers, final_norm_g))
```

```python
import functools
import math
from typing import NamedTuple

import jax
import jax.numpy as jnp
from jax import lax
from jax.experimental import pallas as pl
from jax.experimental.pallas import tpu as pltpu

EPS = 1e-6
B_GROUPS = 8
DFT_RADIX = 4

V7X_VMEM_BYTES = 64 * 1024 * 1024
VMEM_LIMIT_BYTES = V7X_VMEM_BYTES - 8 * 1024 * 1024
LANES = 128
BF16_ROWS = 16
V7X_MXU_DIM = 256

F32 = jnp.float32
BF16 = jnp.bfloat16


def _params(*semantics, flags=None):
    return pltpu.CompilerParams(dimension_semantics=semantics,
                                vmem_limit_bytes=VMEM_LIMIT_BYTES, flags=flags)


def _dot(a, b):
    return jnp.dot(a, b, preferred_element_type=F32)


def _tile(total, preferred):
    t = min(total, preferred)
    assert total % t == 0, (total, preferred)
    return t


def _slices(total, width):
    width = min(width, total)
    assert total % width == 0, (total, width)
    return [slice(c * width, (c + 1) * width) for c in range(total // width)]


def _chunks(shape, rows=2 * V7X_MXU_DIM, cols=V7X_MXU_DIM, tail_rows=None):
    col_slices = _slices(shape[1], cols)
    out = [(r, c) for c in col_slices[:-1] for r in _slices(shape[0], rows)]
    return out + [(r, col_slices[-1]) for r in _slices(shape[0], tail_rows or rows)]


def _scale_rows(acc, r):
    return acc * jnp.tile(r, (1, acc.shape[1] // LANES))


def _row_rsqrt(ssq_lanes, d_model):
    tot = jnp.sum(ssq_lanes, axis=1, keepdims=True)
    return jnp.broadcast_to(lax.rsqrt(tot / d_model + EPS), ssq_lanes.shape)


def _lane_partial_ssq(x):
    sq = x * x
    part = sq[:, :LANES]
    for c in range(1, x.shape[1] // LANES):
        part = part + sq[:, c * LANES:(c + 1) * LANES]
    return part


def _prep_scaled_kernel(w_ref, g_ref, o_ref):
    o_ref[...] = (w_ref[...] * g_ref[...]).astype(o_ref.dtype)


def _prep_plain_kernel(w_ref, o_ref):
    o_ref[...] = w_ref[...].astype(o_ref.dtype)


def _prep_weight(w3, layer, g=None, tk=256):
    _, k, n = w3.shape
    tn = n
    while tn > 5504:
        tn //= 2
    assert n % tn == 0 and tn % LANES == 0 and k % tk == 0
    w_spec = pl.BlockSpec((None, tk, tn), lambda i, j: (layer, i, j))
    o_spec = pl.BlockSpec((tk, tn), lambda i, j: (i, j))
    common = dict(grid=(k // tk, n // tn), out_specs=o_spec,
                  out_shape=jax.ShapeDtypeStruct((k, n), BF16),
                  compiler_params=_params("parallel", "parallel"))
    if g is None:
        return pl.pallas_call(_prep_plain_kernel, in_specs=[w_spec], name="prep_plain", **common)(w3)
    g_spec = pl.BlockSpec((tk, 1), lambda i, j: (i, 0))
    return pl.pallas_call(_prep_scaled_kernel, in_specs=[w_spec, g_spec], name="prep_scaled",
                          **common)(w3, g.reshape(k, 1))


def _cast_ssq_kernel(x_ref, xb_ref, ssq_ref):
    x = x_ref[...]
    xb_ref[...] = x.astype(xb_ref.dtype)
    ssq_ref[...] = _lane_partial_ssq(x)


def _cast_ssq(x, tm=256):
    t, d = x.shape
    row = lambda i: (i, 0)
    return pl.pallas_call(
        _cast_ssq_kernel,
        grid=(t // tm,),
        in_specs=[pl.BlockSpec((tm, d), row)],
        out_specs=[pl.BlockSpec((tm, d), row), pl.BlockSpec((tm, LANES), row)],
        out_shape=[jax.ShapeDtypeStruct((t, d), BF16), jax.ShapeDtypeStruct((t, LANES), F32)],
        compiler_params=_params("parallel"),
        name="cast_ssq",
    )(x)


def _final_norm_kernel(x_ref, ssq_ref, g_ref, o_ref):
    x = x_ref[...]
    o_ref[...] = _scale_rows(x, _row_rsqrt(ssq_ref[...], x.shape[1])) * g_ref[...]


def _final_norm(x, ssq, g, tm=256):
    t, d = x.shape
    row = lambda i: (i, 0)
    return pl.pallas_call(
        _final_norm_kernel,
        grid=(t // tm,),
        in_specs=[pl.BlockSpec((tm, d), row), pl.BlockSpec((tm, LANES), row),
                  pl.BlockSpec((1, d), lambda i: (0, 0))],
        out_specs=pl.BlockSpec((tm, d), row),
        out_shape=jax.ShapeDtypeStruct((t, d), F32),
        compiler_params=_params("parallel"),
        name="final_norm",
    )(x, ssq, g.reshape(1, d))


def _row_scale_kernel(ssq_ref, r_ref, *, d_model):
    r_ref[...] = _row_rsqrt(ssq_ref[...], d_model)


def _row_scale(ssq, d_model, tm=2048):
    t = ssq.shape[0]
    tm = _tile(t, tm)
    spec = pl.BlockSpec((tm, LANES), lambda i: (i, 0))
    return pl.pallas_call(
        functools.partial(_row_scale_kernel, d_model=d_model),
        grid=(t // tm,), in_specs=[spec], out_specs=spec,
        out_shape=jax.ShapeDtypeStruct(ssq.shape, F32),
        compiler_params=_params("parallel"),
        name="row_scale",
    )(ssq)


def _mm_gelu_kernel(a_ref, r_ref, b_ref, o_ref):
    r = r_ref[...]
    for cols in _slices(o_ref.shape[1], V7X_MXU_DIM):
        acc = _scale_rows(_dot(a_ref[...], b_ref[:, cols]), r)
        o_ref[:, cols] = jax.nn.gelu(acc).astype(o_ref.dtype)


def _mm_gelu(a, r, b, col_block_offset, n, out_dtype, tm=1024, tn=1024):
    t, k = a.shape
    tm, tn = _tile(t, tm), _tile(n, tn)
    return pl.pallas_call(
        _mm_gelu_kernel,
        grid=(t // tm, n // tn),
        in_specs=[pl.BlockSpec((tm, k), lambda i, j: (i, 0)),
                  pl.BlockSpec((tm, LANES), lambda i, j: (i, 0)),
                  pl.BlockSpec((k, tn), lambda i, j: (0, j + col_block_offset))],
        out_specs=pl.BlockSpec((tm, tn), lambda i, j: (i, j)),
        out_shape=jax.ShapeDtypeStruct((t, n), out_dtype),
        compiler_params=_params("parallel", "arbitrary"),
        name="mm_gelu",
    )(a, r, b)


class _Pending(NamedTuple):
    w3: jax.Array
    layer: int
    g: jax.Array | None


def _rider_block(pending, k, n, tn, ni, nj):
    rows, cols = pending.w3.shape[1:]
    if k % ni or (k // ni) % BF16_ROWS:
        return None
    if (rows, cols) == (k, n):
        return (k // ni, tn), lambda i, j: (i, j)
    if (rows, cols) == (n, k) and (k // ni) % LANES == 0:
        return (tn, k // ni), lambda i, j: (j, i)
    if (rows, cols) == (k, k) and k % tn == 0 and nj >= k // tn:
        return (k // ni, tn), lambda i, j: (i, jnp.minimum(j, k // tn - 1))
    return None


def _run_riders(in_refs, out_refs, scaled):
    in_refs = iter(in_refs)
    for o_ref, has_g in zip(out_refs, scaled):
        w = next(in_refs)[...]
        if has_g:
            w = w * next(in_refs)[...]
        o_ref[...] = w.astype(o_ref.dtype)


def _mm_swiglu_kernel(a_ref, ssq_ref, wg_ref, wu_ref, *refs, rider_scaled):
    n_rider_in = len(rider_scaled) + sum(rider_scaled)
    rider_in, (o_ref, *rider_out), r_ref = refs[:n_rider_in], refs[n_rider_in:-1], refs[-1]
    _run_riders(rider_in, rider_out, rider_scaled)
    r_ref[...] = _row_rsqrt(ssq_ref[...], a_ref.shape[1])
    for rows, cols in _chunks(o_ref.shape):
        a = a_ref[rows, :]
        r = r_ref[rows, :]
        g = _scale_rows(_dot(a, wg_ref[:, cols]), r)
        u = _scale_rows(_dot(a, wu_ref[:, cols]), r)
        o_ref[rows, cols] = (g * jax.nn.sigmoid(g) * u).astype(o_ref.dtype)


def _mm_swiglu(a, ssq, wg, wu, pending=(), tm=2048, tn=V7X_MXU_DIM):
    t, k = a.shape
    n = wg.shape[1]
    tm, tn = _tile(t, tm), _tile(n, tn)
    ni, nj = t // tm, n // tn
    w_spec = pl.BlockSpec((k, tn), lambda i, j: (0, j))
    operands = [a, ssq, wg, wu]
    in_specs = [pl.BlockSpec((tm, k), lambda i, j: (i, 0)),
                pl.BlockSpec((tm, LANES), lambda i, j: (i, 0)), w_spec, w_spec]
    out_specs = [pl.BlockSpec((tm, tn), lambda i, j: (i, j))]
    out_shape = [jax.ShapeDtypeStruct((t, n), BF16)]
    scaled, fitted = [], []
    for p in pending:
        fit = _rider_block(p, k, n, tn, ni, nj)
        fitted.append(fit is not None)
        if fit is None:
            continue
        (bk, bn), index_map = fit
        operands.append(p.w3)
        in_specs.append(pl.BlockSpec((None, bk, bn),
                                     lambda i, j, p=p, index_map=index_map: (p.layer, *index_map(i, j))))
        if p.g is not None:
            operands.append(p.g.reshape(-1, 1))
            in_specs.append(pl.BlockSpec((bk, 1), lambda i, j, index_map=index_map: (index_map(i, j)[0], 0)))
        scaled.append(p.g is not None)
        out_specs.append(pl.BlockSpec((bk, bn), index_map))
        out_shape.append(jax.ShapeDtypeStruct(p.w3.shape[1:], BF16))
    hidden, *prepared = pl.pallas_call(
        functools.partial(_mm_swiglu_kernel, rider_scaled=tuple(scaled)),
        grid=(ni, nj),
        in_specs=in_specs,
        out_specs=out_specs,
        out_shape=out_shape,
        scratch_shapes=[pltpu.VMEM((tm, LANES), F32)],
        compiler_params=_params("parallel", "arbitrary"),
        name="mm_swiglu",
    )(*operands)
    prepared = iter(prepared)
    return hidden, [next(prepared) if ok else None for ok in fitted]


def _mm_residual_kernel(a_ref, b_ref, res_ref, o_ref, *norm_refs):
    if norm_refs:
        xb_ref, ssq_ref = norm_refs

        @pl.when(pl.program_id(1) == 0)
        def _():
            ssq_ref[...] = jnp.zeros_like(ssq_ref)

    for rows, cols in _chunks(o_ref.shape):
        x_new = res_ref[rows, cols] + _dot(a_ref[rows, :], b_ref[:, cols])
        o_ref[rows, cols] = x_new
        if norm_refs:
            xb_ref[rows, cols] = x_new.astype(xb_ref.dtype)
            ssq_ref[rows, :] += _lane_partial_ssq(x_new)


def _mm_residual(a, b, res, k_block=0, k_blocks=1, emit_norm=False, tm=1024, tn=512):
    t, k = a.shape
    n = b.shape[1]
    tk = k // k_blocks
    tm, tn = _tile(t, tm), _tile(n, tn)
    tile = pl.BlockSpec((tm, tn), lambda i, j: (i, j))
    out_specs = [tile]
    out_shape = [jax.ShapeDtypeStruct((t, n), F32)]
    if emit_norm:
        out_specs += [tile, pl.BlockSpec((tm, LANES), lambda i, j: (i, 0))]
        out_shape += [jax.ShapeDtypeStruct((t, n), BF16), jax.ShapeDtypeStruct((t, LANES), F32)]
    out = pl.pallas_call(
        _mm_residual_kernel,
        grid=(t // tm, n // tn),
        in_specs=[pl.BlockSpec((tm, tk), lambda i, j: (i, k_block)),
                  pl.BlockSpec((tk, tn), lambda i, j: (k_block, j)),
                  tile],
        out_specs=out_specs,
        out_shape=out_shape,
        compiler_params=_params("parallel", "arbitrary"),
        name="mm_residual_norm" if emit_norm else "mm_residual",
    )(a, b, res)
    return out if emit_norm else out[0]


def _sgu_kernel(u_ref, v_ref, g_ref, b_ref, ws_ref, bias_ref, o_ref, vn_ref, *,
                chunk, heads):
    v = v_ref[...]
    mu = jnp.mean(v, axis=-1, keepdims=True)
    vc = v - mu
    r = lax.rsqrt(jnp.mean(vc * vc, axis=-1, keepdims=True) + EPS)
    vn_ref[...] = (vc * r * g_ref[...] + b_ref[...]).astype(vn_ref.dtype)
    hd = v.shape[1] // heads
    for c in range(v.shape[0] // chunk):
        rows = pl.ds(c * chunk, chunk)
        for h in range(heads):
            cols = pl.ds(h * hd, hd)
            s = _dot(ws_ref[h], vn_ref[rows, cols]) + bias_ref[:, cols]
            o_ref[rows, cols] = (u_ref[rows, cols].astype(F32) * s).astype(o_ref.dtype)


def _sgu(u, v, ln_g, ln_b, w_s, b_s, tm=256):
    t, d = v.shape
    heads, chunk, _ = w_s.shape
    bias = jnp.repeat(jnp.transpose(b_s), d // heads, axis=1)
    row = lambda i: (i, 0)
    fixed2 = lambda i: (0, 0)
    return pl.pallas_call(
        functools.partial(_sgu_kernel, chunk=chunk, heads=heads),
        grid=(t // tm,),
        in_specs=[pl.BlockSpec((tm, d), row),
                  pl.BlockSpec((tm, d), row),
                  pl.BlockSpec((1, d), fixed2),
                  pl.BlockSpec((1, d), fixed2),
                  pl.BlockSpec((heads, chunk, chunk), lambda i: (0, 0, 0)),
                  pl.BlockSpec((chunk, d), fixed2)],
        out_specs=pl.BlockSpec((tm, d), row),
        out_shape=jax.ShapeDtypeStruct((t, d), BF16),
        scratch_shapes=[pltpu.VMEM((tm, d), BF16)],
        compiler_params=_params("parallel"),
        name="sgu",
    )(u, v, ln_g.reshape(1, d), ln_b.reshape(1, d), w_s.astype(BF16), bias)


def _dft_tables(n, cols=None, split=64):
    cols = n if cols is None else cols
    split = min(split, n)
    j = jnp.arange(cols, dtype=jnp.int32)[None, :]
    unit = 2.0 * math.pi / n
    a_lo = ((jnp.arange(split, dtype=jnp.int32)[:, None] * j) % n).astype(F32) * unit
    a_hi = (((jnp.arange(n // split, dtype=jnp.int32) * split)[:, None] * j) % n).astype(F32) * unit
    scale = 1.0 / math.sqrt(n)
    c_lo, s_lo = jnp.cos(a_lo)[None, :, :], jnp.sin(a_lo)[None, :, :]
    c_hi, s_hi = (jnp.cos(a_hi) * scale)[:, None, :], (jnp.sin(a_hi) * scale)[:, None, :]
    cos = (c_hi * c_lo - s_hi * s_lo).reshape(n, cols)
    sin = (s_hi * c_lo + c_hi * s_lo).reshape(n, cols)
    return cos, sin


def _position_tables(seq):
    quarter = seq // DFT_RADIX
    cos, sin = _dft_tables(seq, cols=quarter)
    trig = jnp.concatenate([cos, -sin], axis=1).astype(BF16)
    return jnp.transpose(trig.reshape(quarter, DFT_RADIX, 2 * quarter), (1, 0, 2))


def _dft_channels_kernel(h_ref, ssq_ref, cs_ref, pq_ref, r_ref):
    radix, tm, d = h_ref.shape
    groups, gd, _ = cs_ref.shape
    width = min(V7X_MXU_DIM, gd)
    for b in range(radix):
        r_ref[b] = _row_rsqrt(ssq_ref[b], d)
    granules = tm // 2 // BF16_ROWS
    for t0 in (0, tm // 2):
        for g in range(groups):
            gcols = slice(g * gd, (g + 1) * gd)
            lhs = jnp.concatenate(
                [h_ref[b, t0 + u * BF16_ROWS:t0 + (u + 1) * BF16_ROWS, gcols]
                 for u in range(granules) for b in range(radix)], axis=0)
            for si in range(gd // width):
                acc = _dot(lhs, cs_ref[g, :, 2 * si * width:2 * (si + 1) * width])
                cols = slice(g * gd + si * width, g * gd + (si + 1) * width)
                for u in range(granules):
                    rows = slice(t0 + u * BF16_ROWS, t0 + (u + 1) * BF16_ROWS)
                    p, q = [], []
                    for b in range(radix):
                        lo = (u * radix + b) * BF16_ROWS
                        blk = _scale_rows(acc[lo:lo + BF16_ROWS], r_ref[b, rows, :])
                        p.append(blk[:, :width])
                        q.append(blk[:, width:])
                    pa, pb, pc, pd = p[0] + p[2], p[1] + p[3], p[0] - p[2], p[1] - p[3]
                    qa, qb, qc, qd = q[0] + q[2], q[1] + q[3], q[0] - q[2], q[1] - q[3]
                    out = [(pa + pb, qa + qb), (pc - qd, qc + pd), (pa - pb, qa - qb), (pc + qd, qc - pd)]
                    for k, (vp, vq) in enumerate(out):
                        pq_ref[k, 0, rows, cols] = vp.astype(pq_ref.dtype)
                        pq_ref[k, 1, rows, cols] = vq.astype(pq_ref.dtype)


def _dft_channels(h, ssq, cs, batch, tm=128):
    t, d = h.shape
    seq = t // batch
    quarter = seq // DFT_RADIX
    tm = _tile(quarter, tm)
    return pl.pallas_call(
        _dft_channels_kernel,
        grid=(batch, quarter // tm),
        in_specs=[pl.BlockSpec((None, DFT_RADIX, tm, d), lambda b, i: (b, 0, i, 0)),
                  pl.BlockSpec((None, DFT_RADIX, tm, LANES), lambda b, i: (b, 0, i, 0)),
                  pl.BlockSpec(cs.shape, lambda b, i: (0, 0, 0))],
        out_specs=pl.BlockSpec((None, DFT_RADIX, 2, tm, d), lambda b, i: (b, 0, 0, i, 0)),
        out_shape=jax.ShapeDtypeStruct((batch, DFT_RADIX, 2, quarter, d), BF16),
        scratch_shapes=[pltpu.VMEM((DFT_RADIX, tm, LANES), F32)],
        compiler_params=_params("parallel", "parallel"),
        name="dft_channels",
    )(h.reshape(batch, DFT_RADIX, quarter, d), ssq.reshape(batch, DFT_RADIX, quarter, LANES), cs)


def _dft_positions_kernel(trig_ref, pq_ref, o_ref, il_ref):
    radix, quarter, _ = trig_ref.shape
    for ci, cols in enumerate(_slices(o_ref.shape[1], V7X_MXU_DIM)):
        il = il_ref.at[ci % 2]
        lane_groups = _slices(cols.stop - cols.start, LANES)
        for k in range(radix):
            for rows in _slices(quarter, 2 * V7X_MXU_DIM):
                acc = _dot(trig_ref[k, rows, :], pq_ref[k, :, cols])
                dst = pl.ds(radix * rows.start + k, rows.stop - rows.start, stride=radix)
                for c, lanes in enumerate(lane_groups):
                    il[c, dst, :] = acc[:, lanes]
        for c, lanes in enumerate(lane_groups):
            o_ref[:, cols.start + lanes.start:cols.start + lanes.stop] = il[c].astype(o_ref.dtype)


def _dft_positions(trig, pq):
    batch, radix, k2, d = pq.shape
    seq = radix * k2 // 2
    tn = _tile(d, (8 * 1024 * 1024) // (2 * radix * k2))
    return pl.pallas_call(
        _dft_positions_kernel,
        grid=(batch, d // tn),
        in_specs=[pl.BlockSpec(trig.shape, lambda b, j: (0, 0, 0), pipeline_mode=pl.Buffered(1)),
                  pl.BlockSpec((None, radix, k2, tn), lambda b, j: (b, 0, 0, j))],
        out_specs=pl.BlockSpec((None, seq, tn), lambda b, j: (b, 0, j)),
        out_shape=jax.ShapeDtypeStruct((batch, seq, d), BF16),
        scratch_shapes=[pltpu.VMEM((2, V7X_MXU_DIM // LANES, seq, LANES), F32)],
        compiler_params=_params("parallel", "arbitrary"),
        name="dft_positions",
    )(trig, pq)


def _resolve(weights, name):
    if isinstance(weights[name], _Pending):
        weights[name] = _prep_weight(*weights[name])
    return weights[name]


def _ffn(x, xb, ssq, layer, next_layer):
    w_gate, w_up, w_down = (_resolve(layer, n) for n in ("w_gate", "w_up", "w_down"))
    names = [n for n in ("w_gate", "w_up", "w_down", "w_out")
             if next_layer is not None and isinstance(next_layer[n], _Pending)]
    hidden, prepared = _mm_swiglu(xb, ssq, w_gate, w_up, [next_layer[n] for n in names])
    for n, w in zip(names, prepared):
        if w is not None:
            next_layer[n] = w
    x_mid = _mm_residual(hidden, w_down, x, k_block=0, k_blocks=2)
    return _mm_residual(hidden, w_down, x_mid, k_block=1, k_blocks=2, emit_norm=True)


def _gmlp_layer(x, xb, ssq, layer):
    w_in, w_out = _resolve(layer, "w_in"), _resolve(layer, "w_out")
    d_u = w_out.shape[0]
    tn = _tile(d_u, 1024)
    r = _row_scale(ssq, x.shape[1])
    u = _mm_gelu(xb, r, w_in, 0, d_u, BF16, tn=tn)
    v = _mm_gelu(xb, r, w_in, d_u // tn, d_u, F32, tn=tn)
    gated = _sgu(u, v, layer["ln_g"], layer["ln_b"], layer["w_s"], layer["b_s"])
    return _mm_residual(gated, w_out, x, emit_norm=True)


def _fourier_layer(x, xb, ssq, batch, layer):
    t, d = x.shape
    seq = t // batch
    quarter = seq // DFT_RADIX
    cs = _resolve(layer, "cs").reshape(B_GROUPS, d // B_GROUPS, 2 * d // B_GROUPS)
    pq = _dft_channels(xb, ssq, cs, batch)
    y = _dft_positions(_position_tables(seq), pq.reshape(batch, DFT_RADIX, 2 * quarter, d))
    return _mm_residual(y.reshape(t, d), _resolve(layer, "w_out"), x, emit_norm=True)


def _trunk(x3, layers, final_norm_g):
    batch, seq, d = x3.shape
    x = x3.reshape(batch * seq, d)
    xb, ssq = _cast_ssq(x)
    for i, layer in enumerate(layers):
        if i % 2 == 0:
            x, xb, ssq = _gmlp_layer(x, xb, ssq, layer)
        else:
            x, xb, ssq = _fourier_layer(x, xb, ssq, batch, layer)
        x, xb, ssq = _ffn(x, xb, ssq, layer, layers[i + 1] if i + 1 < len(layers) else None)
    return _final_norm(x, ssq, final_norm_g).reshape(batch, seq, d)


def kernel(x_prompt, x_sample, a_norm_g, a_w_in, a_ln_g, a_ln_b, a_w_s, a_b_s, a_w_out,
           b_norm_g, b_w_out, ffn_norm_g, ffn_w_gate, ffn_w_up, ffn_w_down, final_norm_g):
    depth, d, _ = ffn_w_gate.shape
    gd = d // B_GROUPS
    width = min(V7X_MXU_DIM, gd)
    cs_pair = jnp.stack([t.reshape(gd, gd // width, width) for t in _dft_tables(gd)], axis=2)
    cs3 = jnp.tile(cs_pair.reshape(gd, 2 * gd), (B_GROUPS, 1))[None]
    layers = []
    for i in range(depth):
        j = i // 2
        layer = dict(w_gate=_Pending(ffn_w_gate, i, ffn_norm_g[i]),
                     w_up=_Pending(ffn_w_up, i, ffn_norm_g[i]),
                     w_down=_Pending(ffn_w_down, i, None))
        if i % 2 == 0:
            layer.update(w_in=_Pending(a_w_in, j, a_norm_g[j]), w_out=_Pending(a_w_out, j, None),
                         ln_g=a_ln_g[j], ln_b=a_ln_b[j], w_s=a_w_s[j], b_s=a_b_s[j])
        else:
            layer.update(cs=_Pending(cs3, 0, b_norm_g[j]), w_out=_Pending(b_w_out, j, None))
        layers.append(layer)
    return (_trunk(x_prompt, layers, final_norm_g), _trunk(x_sample, layers, final_norm_g))
```

```python
import functools
import math
from typing import NamedTuple

import jax
import jax.numpy as jnp
from jax import lax
from jax.experimental import pallas as pl
from jax.experimental.pallas import tpu as pltpu

EPS = 1e-6
B_GROUPS = 8
DFT_RADIX = 4

V7X_VMEM_BYTES = 64 * 1024 * 1024
VMEM_LIMIT_BYTES = V7X_VMEM_BYTES - 8 * 1024 * 1024
LANES = 128
BF16_ROWS = 16
V7X_MXU_DIM = 256

F32 = jnp.float32
BF16 = jnp.bfloat16


def _params(*semantics, flags=None):
    return pltpu.CompilerParams(dimension_semantics=semantics,
                                vmem_limit_bytes=VMEM_LIMIT_BYTES, flags=flags)


def _dot(a, b):
    return jnp.dot(a, b, preferred_element_type=F32)


def _ceil_div(a, b):
    return -(-a // b)


def _tile(total, preferred):
    t = min(total, preferred)
    assert total % t == 0, (total, preferred)
    return t


def _slices(total, width):
    width = min(width, total)
    assert total % width == 0, (total, width)
    return [slice(c * width, (c + 1) * width) for c in range(total // width)]


def _chunks(shape, rows=2 * V7X_MXU_DIM, cols=V7X_MXU_DIM, tail_rows=None):
    col_slices = _slices(shape[1], cols)
    out = [(r, c) for c in col_slices[:-1] for r in _slices(shape[0], rows)]
    return out + [(r, col_slices[-1]) for r in _slices(shape[0], tail_rows or rows)]


def _scale_rows(acc, r):
    return acc * jnp.tile(r, (1, acc.shape[1] // LANES))


def _row_rsqrt(ssq_lanes, d_model):
    tot = jnp.sum(ssq_lanes, axis=1, keepdims=True)
    return jnp.broadcast_to(lax.rsqrt(tot / d_model + EPS), ssq_lanes.shape)


def _lane_partial_ssq(x):
    sq = x * x
    part = sq[:, :LANES]
    for c in range(1, x.shape[1] // LANES):
        part = part + sq[:, c * LANES:(c + 1) * LANES]
    return part


def _prep_scaled_kernel(w_ref, g_ref, o_ref):
    o_ref[...] = (w_ref[...] * g_ref[...]).astype(o_ref.dtype)


def _prep_plain_kernel(w_ref, o_ref):
    o_ref[...] = w_ref[...].astype(o_ref.dtype)


def _prep_weight(w3, layer, g=None, tk=256):
    _, k, n = w3.shape
    tn = n
    while tn > 5504:
        tn //= 2
    assert n % tn == 0 and tn % LANES == 0 and k % tk == 0
    w_spec = pl.BlockSpec((None, tk, tn), lambda i, j: (layer, i, j))
    o_spec = pl.BlockSpec((tk, tn), lambda i, j: (i, j))
    common = dict(grid=(k // tk, n // tn), out_specs=o_spec,
                  out_shape=jax.ShapeDtypeStruct((k, n), BF16),
                  compiler_params=_params("parallel", "parallel"))
    if g is None:
        return pl.pallas_call(_prep_plain_kernel, in_specs=[w_spec], name="prep_plain", **common)(w3)
    g_spec = pl.BlockSpec((tk, 1), lambda i, j: (i, 0))
    return pl.pallas_call(_prep_scaled_kernel, in_specs=[w_spec, g_spec], name="prep_scaled",
                          **common)(w3, g.reshape(k, 1))


def _cast_ssq_kernel(x_ref, xb_ref, ssq_ref):
    x = x_ref[...]
    xb_ref[...] = x.astype(xb_ref.dtype)
    ssq_ref[...] = _lane_partial_ssq(x)


def _cast_ssq(x, tm=256):
    t, d = x.shape
    row = lambda i: (i, 0)
    return pl.pallas_call(
        _cast_ssq_kernel,
        grid=(t // tm,),
        in_specs=[pl.BlockSpec((tm, d), row)],
        out_specs=[pl.BlockSpec((tm, d), row), pl.BlockSpec((tm, LANES), row)],
        out_shape=[jax.ShapeDtypeStruct((t, d), BF16), jax.ShapeDtypeStruct((t, LANES), F32)],
        compiler_params=_params("parallel"),
        name="cast_ssq",
    )(x)


def _final_norm_kernel(x_ref, ssq_ref, g_ref, o_ref):
    x = x_ref[...]
    o_ref[...] = _scale_rows(x, _row_rsqrt(ssq_ref[...], x.shape[1])) * g_ref[...]


def _final_norm(x, ssq, g, tm=256):
    t, d = x.shape
    row = lambda i: (i, 0)
    return pl.pallas_call(
        _final_norm_kernel,
        grid=(t // tm,),
        in_specs=[pl.BlockSpec((tm, d), row), pl.BlockSpec((tm, LANES), row),
                  pl.BlockSpec((1, d), lambda i: (0, 0))],
        out_specs=pl.BlockSpec((tm, d), row),
        out_shape=jax.ShapeDtypeStruct((t, d), F32),
        compiler_params=_params("parallel"),
        name="final_norm",
    )(x, ssq, g.reshape(1, d))


def _row_scale_kernel(ssq_ref, r_ref, *, d_model):
    r_ref[...] = _row_rsqrt(ssq_ref[...], d_model)


def _row_scale(ssq, d_model, tm=2048):
    t = ssq.shape[0]
    tm = _tile(t, tm)
    spec = pl.BlockSpec((tm, LANES), lambda i: (i, 0))
    return pl.pallas_call(
        functools.partial(_row_scale_kernel, d_model=d_model),
        grid=(t // tm,), in_specs=[spec], out_specs=spec,
        out_shape=jax.ShapeDtypeStruct(ssq.shape, F32),
        compiler_params=_params("parallel"),
        name="row_scale",
    )(ssq)


class _Pending(NamedTuple):
    w3: jax.Array
    layer: int
    g: jax.Array | None


def _rider_block(rows, cols, ni, nj):
    def split(size, parts, align):
        blk = _ceil_div(_ceil_div(size, parts), align) * align
        return blk, _ceil_div(size, blk)

    options = []
    for swap in (False, True):
        parts_r, parts_c = (nj, ni) if swap else (ni, nj)
        (bk, nbk), (bn, nbn) = split(rows, parts_r, BF16_ROWS), split(cols, parts_c, LANES)
        options.append((bk * bn, swap, bk, bn, nbk, nbn))
    _, swap, bk, bn, nbk, nbn = min(options)

    def index_map(i, j):
        ri, ci = (j, i) if swap else (i, j)
        return jnp.minimum(ri, nbk - 1), jnp.minimum(ci, nbn - 1)

    return (bk, bn), index_map


def _attach_riders(pending, ni, nj):
    operands, in_specs, out_specs, out_shape, scaled = [], [], [], [], []
    for p in pending:
        (bk, bn), index_map = _rider_block(*p.w3.shape[1:], ni, nj)
        operands.append(p.w3)
        in_specs.append(pl.BlockSpec((None, bk, bn),
                                     lambda i, j, p=p, index_map=index_map: (p.layer, *index_map(i, j))))
        if p.g is not None:
            operands.append(p.g.reshape(-1, 1))
            in_specs.append(pl.BlockSpec((bk, 1), lambda i, j, index_map=index_map: (index_map(i, j)[0], 0)))
        scaled.append(p.g is not None)
        out_specs.append(pl.BlockSpec((bk, bn), index_map))
        out_shape.append(jax.ShapeDtypeStruct(p.w3.shape[1:], BF16))
    return operands, in_specs, out_specs, out_shape, tuple(scaled)


def _run_riders(in_refs, out_refs, scaled):
    in_refs = iter(in_refs)
    for o_ref, has_g in zip(out_refs, scaled):
        w = next(in_refs)[...]
        if has_g:
            w = w * next(in_refs)[...]
        o_ref[...] = w.astype(o_ref.dtype)


def _n_rider_inputs(rider_scaled):
    return len(rider_scaled) + sum(rider_scaled)


def _mm_gelu_kernel(a_ref, r_ref, b_ref, *refs, rider_scaled):
    n_in = _n_rider_inputs(rider_scaled)
    rider_in, o_ref, rider_out = refs[:n_in], refs[n_in], refs[n_in + 1:]
    _run_riders(rider_in, rider_out, rider_scaled)
    r = r_ref[...]
    for cols in _slices(o_ref.shape[1], V7X_MXU_DIM):
        acc = _scale_rows(_dot(a_ref[...], b_ref[:, cols]), r)
        o_ref[:, cols] = jax.nn.gelu(acc).astype(o_ref.dtype)


def _mm_gelu(a, r, b, col_block_offset, n, out_dtype, pending=(), tm=1024, tn=1024):
    t, k = a.shape
    tm, tn = _tile(t, tm), _tile(n, tn)
    ni, nj = t // tm, n // tn
    r_ops, r_in, r_out, r_shape, scaled = _attach_riders(pending, ni, nj)
    out, *prepared = pl.pallas_call(
        functools.partial(_mm_gelu_kernel, rider_scaled=scaled),
        grid=(ni, nj),
        in_specs=[pl.BlockSpec((tm, k), lambda i, j: (i, 0)),
                  pl.BlockSpec((tm, LANES), lambda i, j: (i, 0)),
                  pl.BlockSpec((k, tn), lambda i, j: (0, j + col_block_offset))] + r_in,
        out_specs=[pl.BlockSpec((tm, tn), lambda i, j: (i, j))] + r_out,
        out_shape=[jax.ShapeDtypeStruct((t, n), out_dtype)] + r_shape,
        compiler_params=_params("parallel", "arbitrary"),
        name="mm_gelu",
    )(a, r, b, *r_ops)
    return out, prepared


def _mm_swiglu_kernel(a_ref, ssq_ref, wg_ref, wu_ref, *refs, rider_scaled):
    n_rider_in = _n_rider_inputs(rider_scaled)
    rider_in, (o_ref, *rider_out), r_ref = refs[:n_rider_in], refs[n_rider_in:-1], refs[-1]
    _run_riders(rider_in, rider_out, rider_scaled)
    r_ref[...] = _row_rsqrt(ssq_ref[...], a_ref.shape[1])
    for rows, cols in _chunks(o_ref.shape):
        a = a_ref[rows, :]
        r = r_ref[rows, :]
        g = _scale_rows(_dot(a, wg_ref[:, cols]), r)
        u = _scale_rows(_dot(a, wu_ref[:, cols]), r)
        o_ref[rows, cols] = (g * jax.nn.sigmoid(g) * u).astype(o_ref.dtype)


def _mm_swiglu(a, ssq, wg, wu, pending=(), tm=2048, tn=V7X_MXU_DIM):
    t, k = a.shape
    n = wg.shape[1]
    tm, tn = _tile(t, tm), _tile(n, tn)
    ni, nj = t // tm, n // tn
    w_spec = pl.BlockSpec((k, tn), lambda i, j: (0, j))
    r_ops, r_in, r_out, r_shape, scaled = _attach_riders(pending, ni, nj)
    hidden, *prepared = pl.pallas_call(
        functools.partial(_mm_swiglu_kernel, rider_scaled=scaled),
        grid=(ni, nj),
        in_specs=[pl.BlockSpec((tm, k), lambda i, j: (i, 0)),
                  pl.BlockSpec((tm, LANES), lambda i, j: (i, 0)), w_spec, w_spec] + r_in,
        out_specs=[pl.BlockSpec((tm, tn), lambda i, j: (i, j))] + r_out,
        out_shape=[jax.ShapeDtypeStruct((t, n), BF16)] + r_shape,
        scratch_shapes=[pltpu.VMEM((tm, LANES), F32)],
        compiler_params=_params("parallel", "arbitrary"),
        name="mm_swiglu",
    )(a, ssq, wg, wu, *r_ops)
    return hidden, prepared


def _mm_residual_kernel(a_ref, b_ref, res_ref, *refs, emit_norm, rider_scaled):
    n_in = _n_rider_inputs(rider_scaled)
    n_out = 3 if emit_norm else 1
    rider_in, o_ref, rider_out = refs[:n_in], refs[n_in], refs[n_in + n_out:]
    _run_riders(rider_in, rider_out, rider_scaled)
    if emit_norm:
        xb_ref, ssq_ref = refs[n_in + 1:n_in + 3]

        @pl.when(pl.program_id(1) == 0)
        def _():
            ssq_ref[...] = jnp.zeros_like(ssq_ref)

    for rows, cols in _chunks(o_ref.shape):
        x_new = res_ref[rows, cols] + _dot(a_ref[rows, :], b_ref[:, cols])
        o_ref[rows, cols] = x_new
        if emit_norm:
            xb_ref[rows, cols] = x_new.astype(xb_ref.dtype)
            ssq_ref[rows, :] += _lane_partial_ssq(x_new)


def _mm_residual(a, b, res, k_start=0, k_size=None, emit_norm=False, pending=(), tm=1024, tn=512):
    t, k = a.shape
    n = b.shape[1]
    k_size = k if k_size is None else k_size
    tm, tn = _tile(t, tm), _tile(n, tn)
    ni, nj = t // tm, n // tn
    tile = pl.BlockSpec((tm, tn), lambda i, j: (i, j))
    out_specs = [tile]
    out_shape = [jax.ShapeDtypeStruct((t, n), F32)]
    if emit_norm:
        out_specs += [tile, pl.BlockSpec((tm, LANES), lambda i, j: (i, 0))]
        out_shape += [jax.ShapeDtypeStruct((t, n), BF16), jax.ShapeDtypeStruct((t, LANES), F32)]
    r_ops, r_in, r_out, r_shape, scaled = _attach_riders(pending, ni, nj)
    out = pl.pallas_call(
        functools.partial(_mm_residual_kernel, emit_norm=emit_norm, rider_scaled=scaled),
        grid=(ni, nj),
        in_specs=[pl.BlockSpec((pl.Element(tm), pl.Element(k_size)), lambda i, j: (i * tm, k_start)),
                  pl.BlockSpec((pl.Element(k_size), pl.Element(tn)), lambda i, j: (k_start, j * tn)),
                  tile] + r_in,
        out_specs=out_specs + r_out,
        out_shape=out_shape + r_shape,
        compiler_params=_params("parallel", "arbitrary"),
        name="mm_residual_norm" if emit_norm else "mm_residual",
    )(a, b, res, *r_ops)
    n_out = len(out_shape)
    return (*out[:n_out], out[n_out:])


def _sgu_kernel(u_ref, v_ref, g_ref, b_ref, ws_ref, bias_ref, o_ref, vn_ref, *,
                chunk, heads):
    v = v_ref[...]
    mu = jnp.mean(v, axis=-1, keepdims=True)
    vc = v - mu
    r = lax.rsqrt(jnp.mean(vc * vc, axis=-1, keepdims=True) + EPS)
    vn_ref[...] = (vc * r * g_ref[...] + b_ref[...]).astype(vn_ref.dtype)
    hd = v.shape[1] // heads
    for c in range(v.shape[0] // chunk):
        rows = pl.ds(c * chunk, chunk)
        for h in range(heads):
            cols = pl.ds(h * hd, hd)
            s = _dot(ws_ref[h], vn_ref[rows, cols]) + bias_ref[:, cols]
            o_ref[rows, cols] = (u_ref[rows, cols].astype(F32) * s).astype(o_ref.dtype)


def _sgu(u, v, ln_g, ln_b, w_s, b_s, tm=256):
    t, d = v.shape
    heads, chunk, _ = w_s.shape
    bias = jnp.repeat(jnp.transpose(b_s), d // heads, axis=1)
    row = lambda i: (i, 0)
    fixed2 = lambda i: (0, 0)
    return pl.pallas_call(
        functools.partial(_sgu_kernel, chunk=chunk, heads=heads),
        grid=(t // tm,),
        in_specs=[pl.BlockSpec((tm, d), row),
                  pl.BlockSpec((tm, d), row),
                  pl.BlockSpec((1, d), fixed2),
                  pl.BlockSpec((1, d), fixed2),
                  pl.BlockSpec((heads, chunk, chunk), lambda i: (0, 0, 0)),
                  pl.BlockSpec((chunk, d), fixed2)],
        out_specs=pl.BlockSpec((tm, d), row),
        out_shape=jax.ShapeDtypeStruct((t, d), BF16),
        scratch_shapes=[pltpu.VMEM((tm, d), BF16)],
        compiler_params=_params("parallel"),
        name="sgu",
    )(u, v, ln_g.reshape(1, d), ln_b.reshape(1, d), w_s.astype(BF16), bias)


def _dft_tables(n, cols=None, split=64):
    cols = n if cols is None else cols
    split = min(split, n)
    j = jnp.arange(cols, dtype=jnp.int32)[None, :]
    unit = 2.0 * math.pi / n
    a_lo = ((jnp.arange(split, dtype=jnp.int32)[:, None] * j) % n).astype(F32) * unit
    a_hi = (((jnp.arange(n // split, dtype=jnp.int32) * split)[:, None] * j) % n).astype(F32) * unit
    scale = 1.0 / math.sqrt(n)
    c_lo, s_lo = jnp.cos(a_lo)[None, :, :], jnp.sin(a_lo)[None, :, :]
    c_hi, s_hi = (jnp.cos(a_hi) * scale)[:, None, :], (jnp.sin(a_hi) * scale)[:, None, :]
    cos = (c_hi * c_lo - s_hi * s_lo).reshape(n, cols)
    sin = (s_hi * c_lo + c_hi * s_lo).reshape(n, cols)
    return cos, sin


def _position_tables(seq):
    quarter = seq // DFT_RADIX
    cos, sin = _dft_tables(seq, cols=quarter)
    trig = jnp.concatenate([cos, -sin], axis=1).astype(BF16)
    return jnp.transpose(trig.reshape(quarter, DFT_RADIX, 2 * quarter), (1, 0, 2))


def _dft_channels_kernel(h_ref, ssq_ref, cs_ref, pq_ref, r_ref):
    radix, tm, d = h_ref.shape
    groups, gd, _ = cs_ref.shape
    width = min(V7X_MXU_DIM, gd)
    for b in range(radix):
        r_ref[b] = _row_rsqrt(ssq_ref[b], d)
    granules = tm // 2 // BF16_ROWS
    for t0 in (0, tm // 2):
        for g in range(groups):
            gcols = slice(g * gd, (g + 1) * gd)
            lhs = jnp.concatenate(
                [h_ref[b, t0 + u * BF16_ROWS:t0 + (u + 1) * BF16_ROWS, gcols]
                 for u in range(granules) for b in range(radix)], axis=0)
            for si in range(gd // width):
                acc = _dot(lhs, cs_ref[g, :, 2 * si * width:2 * (si + 1) * width])
                cols = slice(g * gd + si * width, g * gd + (si + 1) * width)
                for u in range(granules):
                    rows = slice(t0 + u * BF16_ROWS, t0 + (u + 1) * BF16_ROWS)
                    p, q = [], []
                    for b in range(radix):
                        lo = (u * radix + b) * BF16_ROWS
                        blk = _scale_rows(acc[lo:lo + BF16_ROWS], r_ref[b, rows, :])
                        p.append(blk[:, :width])
                        q.append(blk[:, width:])
                    pa, pb, pc, pd = p[0] + p[2], p[1] + p[3], p[0] - p[2], p[1] - p[3]
                    qa, qb, qc, qd = q[0] + q[2], q[1] + q[3], q[0] - q[2], q[1] - q[3]
                    out = [(pa + pb, qa + qb), (pc - qd, qc + pd), (pa - pb, qa - qb), (pc + qd, qc - pd)]
                    for k, (vp, vq) in enumerate(out):
                        pq_ref[k, 0, rows, cols] = vp.astype(pq_ref.dtype)
                        pq_ref[k, 1, rows, cols] = vq.astype(pq_ref.dtype)


def _dft_channels(h, ssq, cs, batch, tm=128):
    t, d = h.shape
    seq = t // batch
    quarter = seq // DFT_RADIX
    tm = _tile(quarter, tm)
    return pl.pallas_call(
        _dft_channels_kernel,
        grid=(batch, quarter // tm),
        in_specs=[pl.BlockSpec((None, DFT_RADIX, tm, d), lambda b, i: (b, 0, i, 0)),
                  pl.BlockSpec((None, DFT_RADIX, tm, LANES), lambda b, i: (b, 0, i, 0)),
                  pl.BlockSpec(cs.shape, lambda b, i: (0, 0, 0))],
        out_specs=pl.BlockSpec((None, DFT_RADIX, 2, tm, d), lambda b, i: (b, 0, 0, i, 0)),
        out_shape=jax.ShapeDtypeStruct((batch, DFT_RADIX, 2, quarter, d), BF16),
        scratch_shapes=[pltpu.VMEM((DFT_RADIX, tm, LANES), F32)],
        compiler_params=_params("parallel", "parallel"),
        name="dft_channels",
    )(h.reshape(batch, DFT_RADIX, quarter, d), ssq.reshape(batch, DFT_RADIX, quarter, LANES), cs)


def _dft_positions_kernel(trig_ref, pq_ref, o_ref, il_ref):
    radix, quarter, _ = trig_ref.shape
    for ci, cols in enumerate(_slices(o_ref.shape[1], V7X_MXU_DIM)):
        il = il_ref.at[ci % 2]
        lane_groups = _slices(cols.stop - cols.start, LANES)
        for k in range(radix):
            for rows in _slices(quarter, 2 * V7X_MXU_DIM):
                acc = _dot(trig_ref[k, rows, :], pq_ref[k, :, cols])
                dst = pl.ds(radix * rows.start + k, rows.stop - rows.start, stride=radix)
                for c, lanes in enumerate(lane_groups):
                    il[c, dst, :] = acc[:, lanes]
        for c, lanes in enumerate(lane_groups):
            o_ref[:, cols.start + lanes.start:cols.start + lanes.stop] = il[c].astype(o_ref.dtype)


def _dft_positions(trig, pq):
    batch, radix, k2, d = pq.shape
    seq = radix * k2 // 2
    tn = _tile(d, (8 * 1024 * 1024) // (2 * radix * k2))
    return pl.pallas_call(
        _dft_positions_kernel,
        grid=(batch, d // tn),
        in_specs=[pl.BlockSpec(trig.shape, lambda b, j: (0, 0, 0), pipeline_mode=pl.Buffered(1)),
                  pl.BlockSpec((None, radix, k2, tn), lambda b, j: (b, 0, 0, j))],
        out_specs=pl.BlockSpec((None, seq, tn), lambda b, j: (b, 0, j)),
        out_shape=jax.ShapeDtypeStruct((batch, seq, d), BF16),
        scratch_shapes=[pltpu.VMEM((2, V7X_MXU_DIM // LANES, seq, LANES), F32)],
        compiler_params=_params("parallel", "arbitrary"),
        name="dft_positions",
    )(trig, pq)


def _resolve(weights, name):
    if isinstance(weights[name], _Pending):
        weights[name] = _prep_weight(*weights[name])
    return weights[name]


def _take_pending(weights, names):
    names = [n for n in names if weights is not None and isinstance(weights.get(n), _Pending)]
    return names, [weights[n] for n in names]


def _ffn(x, xb, ssq, layer, next_layer):
    w_gate, w_up, w_down = (_resolve(layer, n) for n in ("w_gate", "w_up", "w_down"))
    names, pending = _take_pending(next_layer, ("w_gate", "w_up", "w_down", "w_out"))
    hidden, prepared = _mm_swiglu(xb, ssq, w_gate, w_up, pending)
    if names:
        next_layer.update(zip(names, prepared))
    k = w_down.shape[0]
    k_lo = _ceil_div(_ceil_div(k, V7X_MXU_DIM), 2) * V7X_MXU_DIM
    x_mid, _ = _mm_residual(hidden, w_down, x, k_start=0, k_size=k_lo)
    x, xb, ssq, _ = _mm_residual(hidden, w_down, x_mid, k_start=k_lo, k_size=k - k_lo, emit_norm=True)
    return x, xb, ssq


def _gmlp_layer(x, xb, ssq, layer):
    w_in = _resolve(layer, "w_in")
    d_u = layer["w_out"].w3.shape[1] if isinstance(layer["w_out"], _Pending) else layer["w_out"].shape[0]
    tn = _tile(d_u, 1024)
    r = _row_scale(ssq, x.shape[1])
    names, pending = _take_pending(layer, ("w_gate", "w_out"))
    u, prepared = _mm_gelu(xb, r, w_in, 0, d_u, BF16, pending, tn=tn)
    layer.update(zip(names, prepared))
    names, pending = _take_pending(layer, ("w_up",))
    v, prepared = _mm_gelu(xb, r, w_in, d_u // tn, d_u, F32, pending, tn=tn)
    layer.update(zip(names, prepared))
    gated = _sgu(u, v, layer["ln_g"], layer["ln_b"], layer["w_s"], layer["b_s"])
    names, pending = _take_pending(layer, ("w_down",))
    x, xb, ssq, prepared = _mm_residual(gated, layer["w_out"], x, emit_norm=True, pending=pending)
    layer.update(zip(names, prepared))
    return x, xb, ssq


def _fourier_layer(x, xb, ssq, batch, layer):
    t, d = x.shape
    seq = t // batch
    quarter = seq // DFT_RADIX
    cs = _resolve(layer, "cs").reshape(B_GROUPS, d // B_GROUPS, 2 * d // B_GROUPS)
    pq = _dft_channels(xb, ssq, cs, batch)
    y = _dft_positions(_position_tables(seq), pq.reshape(batch, DFT_RADIX, 2 * quarter, d))
    x, xb, ssq, _ = _mm_residual(y.reshape(t, d), _resolve(layer, "w_out"), x, emit_norm=True)
    return x, xb, ssq


def _trunk(x3, layers, final_norm_g):
    batch, seq, d = x3.shape
    x = x3.reshape(batch * seq, d)
    xb, ssq = _cast_ssq(x)
    for i, layer in enumerate(layers):
        if i % 2 == 0:
            x, xb, ssq = _gmlp_layer(x, xb, ssq, layer)
        else:
            x, xb, ssq = _fourier_layer(x, xb, ssq, batch, layer)
        x, xb, ssq = _ffn(x, xb, ssq, layer, layers[i + 1] if i + 1 < len(layers) else None)
    return _final_norm(x, ssq, final_norm_g).reshape(batch, seq, d)


def kernel(x_prompt, x_sample, a_norm_g, a_w_in, a_ln_g, a_ln_b, a_w_s, a_b_s, a_w_out,
           b_norm_g, b_w_out, ffn_norm_g, ffn_w_gate, ffn_w_up, ffn_w_down, final_norm_g):
    depth, d, _ = ffn_w_gate.shape
    gd = d // B_GROUPS
    width = min(V7X_MXU_DIM, gd)
    cs_pair = jnp.stack([t.reshape(gd, gd // width, width) for t in _dft_tables(gd)], axis=2)
    cs3 = jnp.tile(cs_pair.reshape(gd, 2 * gd), (B_GROUPS, 1))[None]
    layers = []
    for i in range(depth):
        j = i // 2
        layer = dict(w_gate=_Pending(ffn_w_gate, i, ffn_norm_g[i]),
                     w_up=_Pending(ffn_w_up, i, ffn_norm_g[i]),
                     w_down=_Pending(ffn_w_down, i, None))
        if i % 2 == 0:
            layer.update(w_in=_Pending(a_w_in, j, a_norm_g[j]), w_out=_Pending(a_w_out, j, None),
                         ln_g=a_ln_g[j], ln_b=a_ln_b[j], w_s=a_w_s[j], b_s=a_b_s[j])
        else:
            layer.update(cs=_Pending(cs3, 0, b_norm_g[j]), w_out=_Pending(b_w_out, j, None))
        layers.append(layer)
    return (_trunk(x_prompt, layers, final_norm_g), _trunk(x_sample, layers, final_norm_g))
```

```python
import functools
import math
from typing import NamedTuple

import jax
import jax.numpy as jnp
from jax import lax
from jax.experimental import pallas as pl
from jax.experimental.pallas import tpu as pltpu

EPS = 1e-6
B_GROUPS = 8
DFT_RADIX = 4

V7X_VMEM_BYTES = 64 * 1024 * 1024
VMEM_LIMIT_BYTES = V7X_VMEM_BYTES - 8 * 1024 * 1024
LANES = 128
BF16_ROWS = 16
V7X_MXU_DIM = 256

F32 = jnp.float32
BF16 = jnp.bfloat16


def _params(*semantics, flags=None):
    return pltpu.CompilerParams(dimension_semantics=semantics,
                                vmem_limit_bytes=VMEM_LIMIT_BYTES, flags=flags)


def _dot(a, b):
    return jnp.dot(a, b, preferred_element_type=F32)


def _ceil_div(a, b):
    return -(-a // b)


def _tile(total, preferred):
    t = min(total, preferred)
    assert total % t == 0, (total, preferred)
    return t


def _slices(total, width):
    width = min(width, total)
    assert total % width == 0, (total, width)
    return [slice(c * width, (c + 1) * width) for c in range(total // width)]


def _chunks(shape, rows=2 * V7X_MXU_DIM, cols=V7X_MXU_DIM, tail_rows=None):
    col_slices = _slices(shape[1], cols)
    out = [(r, c) for c in col_slices[:-1] for r in _slices(shape[0], rows)]
    return out + [(r, col_slices[-1]) for r in _slices(shape[0], tail_rows or rows)]


def _scale_rows(acc, r):
    return acc * jnp.tile(r, (1, acc.shape[1] // LANES))


def _row_rsqrt(ssq_lanes, d_model):
    tot = jnp.sum(ssq_lanes, axis=1, keepdims=True)
    return jnp.broadcast_to(lax.rsqrt(tot / d_model + EPS), ssq_lanes.shape)


def _lane_partial_ssq(x):
    sq = x * x
    part = sq[:, :LANES]
    for c in range(1, x.shape[1] // LANES):
        part = part + sq[:, c * LANES:(c + 1) * LANES]
    return part


def _prep_scaled_kernel(w_ref, g_ref, o_ref):
    o_ref[...] = (w_ref[...] * g_ref[...]).astype(o_ref.dtype)


def _prep_plain_kernel(w_ref, o_ref):
    o_ref[...] = w_ref[...].astype(o_ref.dtype)


def _prep_weight(w3, layer, g=None, tk=256):
    _, k, n = w3.shape
    tn = n
    while tn > 5504:
        tn //= 2
    assert n % tn == 0 and tn % LANES == 0 and k % tk == 0
    w_spec = pl.BlockSpec((None, tk, tn), lambda i, j: (layer, i, j))
    o_spec = pl.BlockSpec((tk, tn), lambda i, j: (i, j))
    common = dict(grid=(k // tk, n // tn), out_specs=o_spec,
                  out_shape=jax.ShapeDtypeStruct((k, n), BF16),
                  compiler_params=_params("parallel", "parallel"))
    if g is None:
        return pl.pallas_call(_prep_plain_kernel, in_specs=[w_spec], name="prep_plain", **common)(w3)
    g_spec = pl.BlockSpec((tk, 1), lambda i, j: (i, 0))
    return pl.pallas_call(_prep_scaled_kernel, in_specs=[w_spec, g_spec], name="prep_scaled",
                          **common)(w3, g.reshape(k, 1))


def _cast_ssq_kernel(x_ref, xb_ref, ssq_ref):
    x = x_ref[...]
    xb_ref[...] = x.astype(xb_ref.dtype)
    ssq_ref[...] = _lane_partial_ssq(x)


def _cast_ssq(x, tm=256):
    t, d = x.shape
    row = lambda i: (i, 0)
    return pl.pallas_call(
        _cast_ssq_kernel,
        grid=(t // tm,),
        in_specs=[pl.BlockSpec((tm, d), row)],
        out_specs=[pl.BlockSpec((tm, d), row), pl.BlockSpec((tm, LANES), row)],
        out_shape=[jax.ShapeDtypeStruct((t, d), BF16), jax.ShapeDtypeStruct((t, LANES), F32)],
        compiler_params=_params("parallel"),
        name="cast_ssq",
    )(x)


def _final_norm_kernel(x_ref, ssq_ref, g_ref, o_ref):
    x = x_ref[...]
    o_ref[...] = _scale_rows(x, _row_rsqrt(ssq_ref[...], x.shape[1])) * g_ref[...]


def _final_norm(x, ssq, g, tm=256):
    t, d = x.shape
    row = lambda i: (i, 0)
    return pl.pallas_call(
        _final_norm_kernel,
        grid=(t // tm,),
        in_specs=[pl.BlockSpec((tm, d), row), pl.BlockSpec((tm, LANES), row),
                  pl.BlockSpec((1, d), lambda i: (0, 0))],
        out_specs=pl.BlockSpec((tm, d), row),
        out_shape=jax.ShapeDtypeStruct((t, d), F32),
        compiler_params=_params("parallel"),
        name="final_norm",
    )(x, ssq, g.reshape(1, d))


def _row_scale_kernel(ssq_ref, r_ref, *, d_model):
    r_ref[...] = _row_rsqrt(ssq_ref[...], d_model)


def _row_scale(ssq, d_model, tm=2048):
    t = ssq.shape[0]
    tm = _tile(t, tm)
    spec = pl.BlockSpec((tm, LANES), lambda i: (i, 0))
    return pl.pallas_call(
        functools.partial(_row_scale_kernel, d_model=d_model),
        grid=(t // tm,), in_specs=[spec], out_specs=spec,
        out_shape=jax.ShapeDtypeStruct(ssq.shape, F32),
        compiler_params=_params("parallel"),
        name="row_scale",
    )(ssq)


class _Pending(NamedTuple):
    w3: jax.Array
    layer: int
    g: jax.Array | None


def _rider_block(rows, cols, ni, nj):
    def split(size, parts, align):
        blk = _ceil_div(_ceil_div(size, parts), align) * align
        return blk, _ceil_div(size, blk)

    options = []
    for swap in (False, True):
        parts_r, parts_c = (nj, ni) if swap else (ni, nj)
        (bk, nbk), (bn, nbn) = split(rows, parts_r, BF16_ROWS), split(cols, parts_c, LANES)
        options.append((bk * bn, swap, bk, bn, nbk, nbn))
    _, swap, bk, bn, nbk, nbn = min(options)

    def index_map(i, j):
        ri, ci = (j, i) if swap else (i, j)
        return jnp.minimum(ri, nbk - 1), jnp.minimum(ci, nbn - 1)

    return (bk, bn), index_map


def _attach_riders(pending, ni, nj):
    operands, in_specs, out_specs, out_shape, scaled = [], [], [], [], []
    for p in pending:
        (bk, bn), index_map = _rider_block(*p.w3.shape[1:], ni, nj)
        operands.append(p.w3)
        in_specs.append(pl.BlockSpec((None, bk, bn),
                                     lambda i, j, p=p, index_map=index_map: (p.layer, *index_map(i, j))))
        if p.g is not None:
            operands.append(p.g.reshape(-1, 1))
            in_specs.append(pl.BlockSpec((bk, 1), lambda i, j, index_map=index_map: (index_map(i, j)[0], 0)))
        scaled.append(p.g is not None)
        out_specs.append(pl.BlockSpec((bk, bn), index_map))
        out_shape.append(jax.ShapeDtypeStruct(p.w3.shape[1:], BF16))
    return operands, in_specs, out_specs, out_shape, tuple(scaled)


def _run_riders(in_refs, out_refs, scaled):
    in_refs = iter(in_refs)
    for o_ref, has_g in zip(out_refs, scaled):
        w = next(in_refs)[...]
        if has_g:
            w = w * next(in_refs)[...]
        o_ref[...] = w.astype(o_ref.dtype)


def _n_rider_inputs(rider_scaled):
    return len(rider_scaled) + sum(rider_scaled)


def _mm_gelu_kernel(a_ref, r_ref, b_ref, *refs, rider_scaled):
    n_in = _n_rider_inputs(rider_scaled)
    rider_in, o_ref, rider_out = refs[:n_in], refs[n_in], refs[n_in + 1:]
    _run_riders(rider_in, rider_out, rider_scaled)
    r = r_ref[...]
    for cols in _slices(o_ref.shape[1], V7X_MXU_DIM):
        acc = _scale_rows(_dot(a_ref[...], b_ref[:, cols]), r)
        o_ref[:, cols] = jax.nn.gelu(acc).astype(o_ref.dtype)


def _mm_gelu(a, r, b, col_block_offset, n, out_dtype, pending=(), tm=1024, tn=1024):
    t, k = a.shape
    tm, tn = _tile(t, tm), _tile(n, tn)
    ni, nj = t // tm, n // tn
    r_ops, r_in, r_out, r_shape, scaled = _attach_riders(pending, ni, nj)
    out, *prepared = pl.pallas_call(
        functools.partial(_mm_gelu_kernel, rider_scaled=scaled),
        grid=(ni, nj),
        in_specs=[pl.BlockSpec((tm, k), lambda i, j: (i, 0)),
                  pl.BlockSpec((tm, LANES), lambda i, j: (i, 0)),
                  pl.BlockSpec((k, tn), lambda i, j: (0, j + col_block_offset))] + r_in,
        out_specs=[pl.BlockSpec((tm, tn), lambda i, j: (i, j))] + r_out,
        out_shape=[jax.ShapeDtypeStruct((t, n), out_dtype)] + r_shape,
        compiler_params=_params("parallel", "arbitrary"),
        name="mm_gelu",
    )(a, r, b, *r_ops)
    return out, prepared


def _mm_swiglu_kernel(a_ref, ssq_ref, wg_ref, wu_ref, *refs, rider_scaled):
    n_rider_in = _n_rider_inputs(rider_scaled)
    rider_in, (o_ref, *rider_out), r_ref = refs[:n_rider_in], refs[n_rider_in:-1], refs[-1]
    _run_riders(rider_in, rider_out, rider_scaled)
    r_ref[...] = _row_rsqrt(ssq_ref[...], a_ref.shape[1])
    for rows, cols in _chunks(o_ref.shape, rows=V7X_MXU_DIM):
        a = a_ref[rows, :]
        r = r_ref[rows, :]
        g = _scale_rows(_dot(a, wg_ref[:, cols]), r)
        u = _scale_rows(_dot(a, wu_ref[:, cols]), r)
        o_ref[rows, cols] = (g * jax.nn.sigmoid(g) * u).astype(o_ref.dtype)


def _mm_swiglu(a, ssq, wg, wu, pending=(), tm=2048, tn=V7X_MXU_DIM):
    t, k = a.shape
    n = wg.shape[1]
    tm, tn = _tile(t, tm), _tile(n, tn)
    ni, nj = t // tm, n // tn
    w_spec = pl.BlockSpec((k, tn), lambda i, j: (0, j))
    r_ops, r_in, r_out, r_shape, scaled = _attach_riders(pending, ni, nj)
    hidden, *prepared = pl.pallas_call(
        functools.partial(_mm_swiglu_kernel, rider_scaled=scaled),
        grid=(ni, nj),
        in_specs=[pl.BlockSpec((tm, k), lambda i, j: (i, 0)),
                  pl.BlockSpec((tm, LANES), lambda i, j: (i, 0)), w_spec, w_spec] + r_in,
        out_specs=[pl.BlockSpec((tm, tn), lambda i, j: (i, j))] + r_out,
        out_shape=[jax.ShapeDtypeStruct((t, n), BF16)] + r_shape,
        scratch_shapes=[pltpu.VMEM((tm, LANES), F32)],
        compiler_params=_params("parallel", "arbitrary"),
        name="mm_swiglu",
    )(a, ssq, wg, wu, *r_ops)
    return hidden, prepared


def _mm_residual_kernel(a_ref, b_ref, res_ref, *refs, emit_norm, rider_scaled):
    n_in = _n_rider_inputs(rider_scaled)
    n_out = 3 if emit_norm else 1
    rider_in, o_ref, rider_out = refs[:n_in], refs[n_in], refs[n_in + n_out:]
    _run_riders(rider_in, rider_out, rider_scaled)
    if emit_norm:
        xb_ref, ssq_ref = refs[n_in + 1:n_in + 3]

        @pl.when(pl.program_id(1) == 0)
        def _():
            ssq_ref[...] = jnp.zeros_like(ssq_ref)

    for rows, cols in _chunks(o_ref.shape):
        x_new = res_ref[rows, cols] + _dot(a_ref[rows, :], b_ref[:, cols])
        o_ref[rows, cols] = x_new
        if emit_norm:
            xb_ref[rows, cols] = x_new.astype(xb_ref.dtype)
            ssq_ref[rows, :] += _lane_partial_ssq(x_new)


def _mm_residual(a, b, res, k_start=0, k_size=None, emit_norm=False, pending=()):
    t, k = a.shape
    n = b.shape[1]
    k_size = k if k_size is None else k_size
    tm, tn = (2048, 256) if 2 * 2048 * k_size * 2 <= V7X_VMEM_BYTES // 2 else (1024, 512)
    tm, tn = _tile(t, tm), _tile(n, tn)
    ni, nj = t // tm, n // tn
    tile = pl.BlockSpec((tm, tn), lambda i, j: (i, j))
    out_specs = [tile]
    out_shape = [jax.ShapeDtypeStruct((t, n), F32)]
    if emit_norm:
        out_specs += [tile, pl.BlockSpec((tm, LANES), lambda i, j: (i, 0))]
        out_shape += [jax.ShapeDtypeStruct((t, n), BF16), jax.ShapeDtypeStruct((t, LANES), F32)]
    r_ops, r_in, r_out, r_shape, scaled = _attach_riders(pending, ni, nj)
    out = pl.pallas_call(
        functools.partial(_mm_residual_kernel, emit_norm=emit_norm, rider_scaled=scaled),
        grid=(ni, nj),
        in_specs=[pl.BlockSpec((pl.Element(tm), pl.Element(k_size)), lambda i, j: (i * tm, k_start)),
                  pl.BlockSpec((pl.Element(k_size), pl.Element(tn)), lambda i, j: (k_start, j * tn)),
                  tile] + r_in,
        out_specs=out_specs + r_out,
        out_shape=out_shape + r_shape,
        compiler_params=_params("parallel", "arbitrary"),
        name="mm_residual_norm" if emit_norm else "mm_residual",
    )(a, b, res, *r_ops)
    n_out = len(out_shape)
    return (*out[:n_out], out[n_out:])


def _sgu_kernel(u_ref, v_ref, g_ref, b_ref, ws_ref, bias_ref, o_ref, vn_ref, *,
                chunk, heads):
    v = v_ref[...]
    mu = jnp.mean(v, axis=-1, keepdims=True)
    vc = v - mu
    r = lax.rsqrt(jnp.mean(vc * vc, axis=-1, keepdims=True) + EPS)
    vn_ref[...] = (vc * r * g_ref[...] + b_ref[...]).astype(vn_ref.dtype)
    hd = v.shape[1] // heads
    for c in range(v.shape[0] // chunk):
        rows = pl.ds(c * chunk, chunk)
        for h in range(heads):
            cols = pl.ds(h * hd, hd)
            s = _dot(ws_ref[h], vn_ref[rows, cols]) + bias_ref[:, cols]
            o_ref[rows, cols] = (u_ref[rows, cols].astype(F32) * s).astype(o_ref.dtype)


def _sgu(u, v, ln_g, ln_b, w_s, b_s, tm=256):
    t, d = v.shape
    heads, chunk, _ = w_s.shape
    bias = jnp.repeat(jnp.transpose(b_s), d // heads, axis=1)
    row = lambda i: (i, 0)
    fixed2 = lambda i: (0, 0)
    return pl.pallas_call(
        functools.partial(_sgu_kernel, chunk=chunk, heads=heads),
        grid=(t // tm,),
        in_specs=[pl.BlockSpec((tm, d), row),
                  pl.BlockSpec((tm, d), row),
                  pl.BlockSpec((1, d), fixed2),
                  pl.BlockSpec((1, d), fixed2),
                  pl.BlockSpec((heads, chunk, chunk), lambda i: (0, 0, 0)),
                  pl.BlockSpec((chunk, d), fixed2)],
        out_specs=pl.BlockSpec((tm, d), row),
        out_shape=jax.ShapeDtypeStruct((t, d), BF16),
        scratch_shapes=[pltpu.VMEM((tm, d), BF16)],
        compiler_params=_params("parallel"),
        name="sgu",
    )(u, v, ln_g.reshape(1, d), ln_b.reshape(1, d), w_s.astype(BF16), bias)


def _dft_tables(n, cols=None, split=64):
    cols = n if cols is None else cols
    split = min(split, n)
    j = jnp.arange(cols, dtype=jnp.int32)[None, :]
    unit = 2.0 * math.pi / n
    a_lo = ((jnp.arange(split, dtype=jnp.int32)[:, None] * j) % n).astype(F32) * unit
    a_hi = (((jnp.arange(n // split, dtype=jnp.int32) * split)[:, None] * j) % n).astype(F32) * unit
    scale = 1.0 / math.sqrt(n)
    c_lo, s_lo = jnp.cos(a_lo)[None, :, :], jnp.sin(a_lo)[None, :, :]
    c_hi, s_hi = (jnp.cos(a_hi) * scale)[:, None, :], (jnp.sin(a_hi) * scale)[:, None, :]
    cos = (c_hi * c_lo - s_hi * s_lo).reshape(n, cols)
    sin = (s_hi * c_lo + c_hi * s_lo).reshape(n, cols)
    return cos, sin


def _position_tables(seq):
    quarter = seq // DFT_RADIX
    cos, sin = _dft_tables(seq, cols=quarter)
    trig = jnp.concatenate([cos, -sin], axis=1).astype(BF16)
    return jnp.transpose(trig.reshape(quarter, DFT_RADIX, 2 * quarter), (1, 0, 2))


def _dft_channels_kernel(h_ref, ssq_ref, cs_ref, pq_ref, r_ref):
    radix, tm, d = h_ref.shape
    groups, gd, _ = cs_ref.shape
    width = min(V7X_MXU_DIM, gd)
    for b in range(radix):
        r_ref[b] = _row_rsqrt(ssq_ref[b], d)
    granules = tm // 2 // BF16_ROWS
    for t0 in (0, tm // 2):
        for g in range(groups):
            gcols = slice(g * gd, (g + 1) * gd)
            lhs = jnp.concatenate(
                [h_ref[b, t0 + u * BF16_ROWS:t0 + (u + 1) * BF16_ROWS, gcols]
                 for u in range(granules) for b in range(radix)], axis=0)
            for si in range(gd // width):
                acc = _dot(lhs, cs_ref[g, :, 2 * si * width:2 * (si + 1) * width])
                cols = slice(g * gd + si * width, g * gd + (si + 1) * width)
                for u in range(granules):
                    rows = slice(t0 + u * BF16_ROWS, t0 + (u + 1) * BF16_ROWS)
                    p, q = [], []
                    for b in range(radix):
                        lo = (u * radix + b) * BF16_ROWS
                        blk = _scale_rows(acc[lo:lo + BF16_ROWS], r_ref[b, rows, :])
                        p.append(blk[:, :width])
                        q.append(blk[:, width:])
                    pa, pb, pc, pd = p[0] + p[2], p[1] + p[3], p[0] - p[2], p[1] - p[3]
                    qa, qb, qc, qd = q[0] + q[2], q[1] + q[3], q[0] - q[2], q[1] - q[3]
                    out = [(pa + pb, qa + qb), (pc - qd, qc + pd), (pa - pb, qa - qb), (pc + qd, qc - pd)]
                    for k, (vp, vq) in enumerate(out):
                        pq_ref[k, 0, rows, cols] = vp.astype(pq_ref.dtype)
                        pq_ref[k, 1, rows, cols] = vq.astype(pq_ref.dtype)


def _dft_channels(h, ssq, cs, batch, tm=128):
    t, d = h.shape
    seq = t // batch
    quarter = seq // DFT_RADIX
    tm = _tile(quarter, tm)
    return pl.pallas_call(
        _dft_channels_kernel,
        grid=(batch, quarter // tm),
        in_specs=[pl.BlockSpec((None, DFT_RADIX, tm, d), lambda b, i: (b, 0, i, 0)),
                  pl.BlockSpec((None, DFT_RADIX, tm, LANES), lambda b, i: (b, 0, i, 0)),
                  pl.BlockSpec(cs.shape, lambda b, i: (0, 0, 0))],
        out_specs=pl.BlockSpec((None, DFT_RADIX, 2, tm, d), lambda b, i: (b, 0, 0, i, 0)),
        out_shape=jax.ShapeDtypeStruct((batch, DFT_RADIX, 2, quarter, d), BF16),
        scratch_shapes=[pltpu.VMEM((DFT_RADIX, tm, LANES), F32)],
        compiler_params=_params("parallel", "parallel"),
        name="dft_channels",
    )(h.reshape(batch, DFT_RADIX, quarter, d), ssq.reshape(batch, DFT_RADIX, quarter, LANES), cs)


def _dft_positions_kernel(trig_ref, pq_ref, o_ref, il_ref):
    radix, quarter, _ = trig_ref.shape
    for ci, cols in enumerate(_slices(o_ref.shape[1], V7X_MXU_DIM)):
        il = il_ref.at[ci % 2]
        lane_groups = _slices(cols.stop - cols.start, LANES)
        for k in range(radix):
            for rows in _slices(quarter, 2 * V7X_MXU_DIM):
                acc = _dot(trig_ref[k, rows, :], pq_ref[k, :, cols])
                dst = pl.ds(radix * rows.start + k, rows.stop - rows.start, stride=radix)
                for c, lanes in enumerate(lane_groups):
                    il[c, dst, :] = acc[:, lanes]
        for c, lanes in enumerate(lane_groups):
            o_ref[:, cols.start + lanes.start:cols.start + lanes.stop] = il[c].astype(o_ref.dtype)


def _dft_positions(trig, pq):
    batch, radix, k2, d = pq.shape
    seq = radix * k2 // 2
    tn = _tile(d, (8 * 1024 * 1024) // (2 * radix * k2))
    return pl.pallas_call(
        _dft_positions_kernel,
        grid=(batch, d // tn),
        in_specs=[pl.BlockSpec(trig.shape, lambda b, j: (0, 0, 0), pipeline_mode=pl.Buffered(1)),
                  pl.BlockSpec((None, radix, k2, tn), lambda b, j: (b, 0, 0, j))],
        out_specs=pl.BlockSpec((None, seq, tn), lambda b, j: (b, 0, j)),
        out_shape=jax.ShapeDtypeStruct((batch, seq, d), BF16),
        scratch_shapes=[pltpu.VMEM((2, V7X_MXU_DIM // LANES, seq, LANES), F32)],
        compiler_params=_params("parallel", "arbitrary"),
        name="dft_positions",
    )(trig, pq)


def _resolve(weights, name):
    if isinstance(weights[name], _Pending):
        weights[name] = _prep_weight(*weights[name])
    return weights[name]


def _take_pending(weights, names):
    names = [n for n in names if weights is not None and isinstance(weights.get(n), _Pending)]
    return names, [weights[n] for n in names]


def _ffn(x, xb, ssq, layer, next_layer):
    w_gate, w_up, w_down = (_resolve(layer, n) for n in ("w_gate", "w_up", "w_down"))
    names, pending = _take_pending(next_layer, ("w_gate", "w_up", "w_down", "w_out"))
    hidden, prepared = _mm_swiglu(xb, ssq, w_gate, w_up, pending)
    if names:
        next_layer.update(zip(names, prepared))
    k = w_down.shape[0]
    k_lo = _ceil_div(_ceil_div(k, V7X_MXU_DIM), 2) * V7X_MXU_DIM
    x_mid, _ = _mm_residual(hidden, w_down, x, k_start=0, k_size=k_lo)
    x, xb, ssq, _ = _mm_residual(hidden, w_down, x_mid, k_start=k_lo, k_size=k - k_lo, emit_norm=True)
    return x, xb, ssq


def _gmlp_layer(x, xb, ssq, layer):
    w_in = _resolve(layer, "w_in")
    d_u = layer["w_out"].w3.shape[1] if isinstance(layer["w_out"], _Pending) else layer["w_out"].shape[0]
    tn = _tile(d_u, 1024)
    r = _row_scale(ssq, x.shape[1])
    names, pending = _take_pending(layer, ("w_gate", "w_out"))
    u, prepared = _mm_gelu(xb, r, w_in, 0, d_u, BF16, pending, tn=tn)
    layer.update(zip(names, prepared))
    names, pending = _take_pending(layer, ("w_up",))
    v, prepared = _mm_gelu(xb, r, w_in, d_u // tn, d_u, F32, pending, tn=tn)
    layer.update(zip(names, prepared))
    gated = _sgu(u, v, layer["ln_g"], layer["ln_b"], layer["w_s"], layer["b_s"])
    names, pending = _take_pending(layer, ("w_down",))
    x, xb, ssq, prepared = _mm_residual(gated, layer["w_out"], x, emit_norm=True, pending=pending)
    layer.update(zip(names, prepared))
    return x, xb, ssq


def _fourier_layer(x, xb, ssq, batch, layer):
    t, d = x.shape
    seq = t // batch
    quarter = seq // DFT_RADIX
    cs = _resolve(layer, "cs").reshape(B_GROUPS, d // B_GROUPS, 2 * d // B_GROUPS)
    pq = _dft_channels(xb, ssq, cs, batch)
    y = _dft_positions(_position_tables(seq), pq.reshape(batch, DFT_RADIX, 2 * quarter, d))
    x, xb, ssq, _ = _mm_residual(y.reshape(t, d), _resolve(layer, "w_out"), x, emit_norm=True)
    return x, xb, ssq


def _trunk(x3, layers, final_norm_g):
    batch, seq, d = x3.shape
    x = x3.reshape(batch * seq, d)
    xb, ssq = _cast_ssq(x)
    for i, layer in enumerate(layers):
        if i % 2 == 0:
            x, xb, ssq = _gmlp_layer(x, xb, ssq, layer)
        else:
            x, xb, ssq = _fourier_layer(x, xb, ssq, batch, layer)
        x, xb, ssq = _ffn(x, xb, ssq, layer, layers[i + 1] if i + 1 < len(layers) else None)
    return _final_norm(x, ssq, final_norm_g).reshape(batch, seq, d)


def kernel(x_prompt, x_sample, a_norm_g, a_w_in, a_ln_g, a_ln_b, a_w_s, a_b_s, a_w_out,
           b_norm_g, b_w_out, ffn_norm_g, ffn_w_gate, ffn_w_up, ffn_w_down, final_norm_g):
    depth, d, _ = ffn_w_gate.shape
    gd = d // B_GROUPS
    width = min(V7X_MXU_DIM, gd)
    cs_pair = jnp.stack([t.reshape(gd, gd // width, width) for t in _dft_tables(gd)], axis=2)
    cs3 = jnp.tile(cs_pair.reshape(gd, 2 * gd), (B_GROUPS, 1))[None]
    layers = []
    for i in range(depth):
        j = i // 2
        layer = dict(w_gate=_Pending(ffn_w_gate, i, ffn_norm_g[i]),
                     w_up=_Pending(ffn_w_up, i, ffn_norm_g[i]),
                     w_down=_Pending(ffn_w_down, i, None))
        if i % 2 == 0:
            layer.update(w_in=_Pending(a_w_in, j, a_norm_g[j]), w_out=_Pending(a_w_out, j, None),
                         ln_g=a_ln_g[j], ln_b=a_ln_b[j], w_s=a_w_s[j], b_s=a_b_s[j])
        else:
            layer.update(cs=_Pending(cs3, 0, b_norm_g[j]), w_out=_Pending(b_w_out, j, None))
        layers.append(layer)
    return (_trunk(x_prompt, layers, final_norm_g), _trunk(x_sample, layers, final_norm_g))
```

```python
import functools
import math
from typing import NamedTuple

import jax
import jax.numpy as jnp
from jax import lax
from jax.experimental import pallas as pl
from jax.experimental.pallas import tpu as pltpu

EPS = 1e-6
B_GROUPS = 8
DFT_RADIX = 4

V7X_VMEM_BYTES = 64 * 1024 * 1024
VMEM_LIMIT_BYTES = V7X_VMEM_BYTES - 8 * 1024 * 1024
LANES = 128
BF16_ROWS = 16
V7X_MXU_DIM = 256

F32 = jnp.float32
BF16 = jnp.bfloat16


def _params(*semantics, flags=None):
    return pltpu.CompilerParams(dimension_semantics=semantics,
                                vmem_limit_bytes=VMEM_LIMIT_BYTES, flags=flags)


def _dot(a, b):
    return jnp.dot(a, b, preferred_element_type=F32)


def _ceil_div(a, b):
    return -(-a // b)


def _tile(total, preferred):
    t = min(total, preferred)
    assert total % t == 0, (total, preferred)
    return t


def _slices(total, width):
    width = min(width, total)
    assert total % width == 0, (total, width)
    return [slice(c * width, (c + 1) * width) for c in range(total // width)]


def _chunks(shape, rows=2 * V7X_MXU_DIM, cols=V7X_MXU_DIM, tail_rows=None):
    col_slices = _slices(shape[1], cols)
    out = [(r, c) for c in col_slices[:-1] for r in _slices(shape[0], rows)]
    return out + [(r, col_slices[-1]) for r in _slices(shape[0], tail_rows or rows)]


def _scale_rows(acc, r):
    return acc * jnp.tile(r, (1, acc.shape[1] // LANES))


def _row_rsqrt(ssq_lanes, d_model):
    tot = jnp.sum(ssq_lanes, axis=1, keepdims=True)
    return jnp.broadcast_to(lax.rsqrt(tot / d_model + EPS), ssq_lanes.shape)


def _lane_partial_ssq(x):
    sq = x * x
    part = sq[:, :LANES]
    for c in range(1, x.shape[1] // LANES):
        part = part + sq[:, c * LANES:(c + 1) * LANES]
    return part


def _prep_scaled_kernel(w_ref, g_ref, o_ref):
    o_ref[...] = (w_ref[...] * g_ref[...]).astype(o_ref.dtype)


def _prep_plain_kernel(w_ref, o_ref):
    o_ref[...] = w_ref[...].astype(o_ref.dtype)


def _prep_weight(w3, layer, g=None, tk=256):
    _, k, n = w3.shape
    tn = n
    while tn > 5504:
        tn //= 2
    assert n % tn == 0 and tn % LANES == 0 and k % tk == 0
    w_spec = pl.BlockSpec((None, tk, tn), lambda i, j: (layer, i, j))
    o_spec = pl.BlockSpec((tk, tn), lambda i, j: (i, j))
    common = dict(grid=(k // tk, n // tn), out_specs=o_spec,
                  out_shape=jax.ShapeDtypeStruct((k, n), BF16),
                  compiler_params=_params("parallel", "parallel"))
    if g is None:
        return pl.pallas_call(_prep_plain_kernel, in_specs=[w_spec], name="prep_plain", **common)(w3)
    g_spec = pl.BlockSpec((tk, 1), lambda i, j: (i, 0))
    return pl.pallas_call(_prep_scaled_kernel, in_specs=[w_spec, g_spec], name="prep_scaled",
                          **common)(w3, g.reshape(k, 1))


def _cast_ssq_kernel(x_ref, xb_ref, ssq_ref):
    x = x_ref[...]
    xb_ref[...] = x.astype(xb_ref.dtype)
    ssq_ref[...] = _lane_partial_ssq(x)


def _cast_ssq(x, tm=512):
    t, d = x.shape
    row = lambda i: (i, 0)
    return pl.pallas_call(
        _cast_ssq_kernel,
        grid=(t // tm,),
        in_specs=[pl.BlockSpec((tm, d), row)],
        out_specs=[pl.BlockSpec((tm, d), row), pl.BlockSpec((tm, LANES), row)],
        out_shape=[jax.ShapeDtypeStruct((t, d), BF16), jax.ShapeDtypeStruct((t, LANES), F32)],
        compiler_params=_params("parallel"),
        name="cast_ssq",
    )(x)


def _final_norm_kernel(x_ref, ssq_ref, g_ref, o_ref):
    x = x_ref[...]
    o_ref[...] = _scale_rows(x, _row_rsqrt(ssq_ref[...], x.shape[1])) * g_ref[...]


def _final_norm(x, ssq, g, tm=512):
    t, d = x.shape
    row = lambda i: (i, 0)
    return pl.pallas_call(
        _final_norm_kernel,
        grid=(t // tm,),
        in_specs=[pl.BlockSpec((tm, d), row), pl.BlockSpec((tm, LANES), row),
                  pl.BlockSpec((1, d), lambda i: (0, 0))],
        out_specs=pl.BlockSpec((tm, d), row),
        out_shape=jax.ShapeDtypeStruct((t, d), F32),
        compiler_params=_params("parallel"),
        name="final_norm",
    )(x, ssq, g.reshape(1, d))


def _row_scale_kernel(ssq_ref, r_ref, *, d_model):
    r_ref[...] = _row_rsqrt(ssq_ref[...], d_model)


def _row_scale(ssq, d_model, tm=2048):
    t = ssq.shape[0]
    tm = _tile(t, tm)
    spec = pl.BlockSpec((tm, LANES), lambda i: (i, 0))
    return pl.pallas_call(
        functools.partial(_row_scale_kernel, d_model=d_model),
        grid=(t // tm,), in_specs=[spec], out_specs=spec,
        out_shape=jax.ShapeDtypeStruct(ssq.shape, F32),
        compiler_params=_params("parallel"),
        name="row_scale",
    )(ssq)


class _Pending(NamedTuple):
    w3: jax.Array
    layer: int
    g: jax.Array | None


def _rider_block(rows, cols, ni, nj):
    def split(size, parts, align):
        blk = _ceil_div(_ceil_div(size, parts), align) * align
        return blk, _ceil_div(size, blk)

    options = []
    for swap in (False, True):
        parts_r, parts_c = (nj, ni) if swap else (ni, nj)
        (bk, nbk), (bn, nbn) = split(rows, parts_r, BF16_ROWS), split(cols, parts_c, LANES)
        options.append((bk * bn, swap, bk, bn, nbk, nbn))
    _, swap, bk, bn, nbk, nbn = min(options)

    def index_map(i, j):
        ri, ci = (j, i) if swap else (i, j)
        return jnp.minimum(ri, nbk - 1), jnp.minimum(ci, nbn - 1)

    return (bk, bn), index_map


def _attach_riders(pending, ni, nj):
    operands, in_specs, out_specs, out_shape, scaled = [], [], [], [], []
    for p in pending:
        (bk, bn), index_map = _rider_block(*p.w3.shape[1:], ni, nj)
        operands.append(p.w3)
        in_specs.append(pl.BlockSpec((None, bk, bn),
                                     lambda i, j, p=p, index_map=index_map: (p.layer, *index_map(i, j))))
        if p.g is not None:
            operands.append(p.g.reshape(-1, 1))
            in_specs.append(pl.BlockSpec((bk, 1), lambda i, j, index_map=index_map: (index_map(i, j)[0], 0)))
        scaled.append(p.g is not None)
        out_specs.append(pl.BlockSpec((bk, bn), index_map))
        out_shape.append(jax.ShapeDtypeStruct(p.w3.shape[1:], BF16))
    return operands, in_specs, out_specs, out_shape, tuple(scaled)


def _run_riders(in_refs, out_refs, scaled):
    in_refs = iter(in_refs)
    for o_ref, has_g in zip(out_refs, scaled):
        w = next(in_refs)[...]
        if has_g:
            w = w * next(in_refs)[...]
        o_ref[...] = w.astype(o_ref.dtype)


def _n_rider_inputs(rider_scaled):
    return len(rider_scaled) + sum(rider_scaled)


def _mm_gelu_kernel(a_ref, r_ref, b_ref, *refs, rider_scaled):
    n_in = _n_rider_inputs(rider_scaled)
    rider_in, o_ref, rider_out = refs[:n_in], refs[n_in], refs[n_in + 1:]
    _run_riders(rider_in, rider_out, rider_scaled)
    r = r_ref[...]
    for cols in _slices(o_ref.shape[1], V7X_MXU_DIM):
        acc = _scale_rows(_dot(a_ref[...], b_ref[:, cols]), r)
        o_ref[:, cols] = jax.nn.gelu(acc).astype(o_ref.dtype)


def _mm_gelu(a, r, b, col_block_offset, n, out_dtype, pending=(), tm=1024, tn=1024):
    t, k = a.shape
    tm, tn = _tile(t, tm), _tile(n, tn)
    ni, nj = t // tm, n // tn
    r_ops, r_in, r_out, r_shape, scaled = _attach_riders(pending, ni, nj)
    out, *prepared = pl.pallas_call(
        functools.partial(_mm_gelu_kernel, rider_scaled=scaled),
        grid=(ni, nj),
        in_specs=[pl.BlockSpec((tm, k), lambda i, j: (i, 0)),
                  pl.BlockSpec((tm, LANES), lambda i, j: (i, 0)),
                  pl.BlockSpec((k, tn), lambda i, j: (0, j + col_block_offset))] + r_in,
        out_specs=[pl.BlockSpec((tm, tn), lambda i, j: (i, j))] + r_out,
        out_shape=[jax.ShapeDtypeStruct((t, n), out_dtype)] + r_shape,
        compiler_params=_params("parallel", "arbitrary"),
        name="mm_gelu",
    )(a, r, b, *r_ops)
    return out, prepared


def _mm_swiglu_kernel(a_ref, ssq_ref, wg_ref, wu_ref, *refs, rider_scaled):
    n_rider_in = _n_rider_inputs(rider_scaled)
    rider_in, (o_ref, *rider_out), r_ref = refs[:n_rider_in], refs[n_rider_in:-1], refs[-1]
    _run_riders(rider_in, rider_out, rider_scaled)
    r_ref[...] = _row_rsqrt(ssq_ref[...], a_ref.shape[1])
    for rows, cols in _chunks(o_ref.shape, rows=V7X_MXU_DIM):
        a = a_ref[rows, :]
        r = r_ref[rows, :]
        g = _scale_rows(_dot(a, wg_ref[:, cols]), r)
        u = _scale_rows(_dot(a, wu_ref[:, cols]), r)
        o_ref[rows, cols] = (g * jax.nn.sigmoid(g) * u).astype(o_ref.dtype)


def _mm_swiglu(a, ssq, wg, wu, pending=(), tm=2048, tn=V7X_MXU_DIM):
    t, k = a.shape
    n = wg.shape[1]
    tm, tn = _tile(t, tm), _tile(n, tn)
    ni, nj = t // tm, n // tn
    w_spec = pl.BlockSpec((k, tn), lambda i, j: (0, j))
    r_ops, r_in, r_out, r_shape, scaled = _attach_riders(pending, ni, nj)
    hidden, *prepared = pl.pallas_call(
        functools.partial(_mm_swiglu_kernel, rider_scaled=scaled),
        grid=(ni, nj),
        in_specs=[pl.BlockSpec((tm, k), lambda i, j: (i, 0)),
                  pl.BlockSpec((tm, LANES), lambda i, j: (i, 0)), w_spec, w_spec] + r_in,
        out_specs=[pl.BlockSpec((tm, tn), lambda i, j: (i, j))] + r_out,
        out_shape=[jax.ShapeDtypeStruct((t, n), BF16)] + r_shape,
        scratch_shapes=[pltpu.VMEM((tm, LANES), F32)],
        compiler_params=_params("parallel", "arbitrary"),
        name="mm_swiglu",
    )(a, ssq, wg, wu, *r_ops)
    return hidden, prepared


def _mm_residual_kernel(a_ref, b_ref, res_ref, *refs, emit_norm, rider_scaled):
    n_in = _n_rider_inputs(rider_scaled)
    n_out = 3 if emit_norm else 1
    rider_in, o_ref, rider_out = refs[:n_in], refs[n_in], refs[n_in + n_out:]
    _run_riders(rider_in, rider_out, rider_scaled)
    if emit_norm:
        xb_ref, ssq_ref = refs[n_in + 1:n_in + 3]

        @pl.when(pl.program_id(1) == 0)
        def _():
            ssq_ref[...] = jnp.zeros_like(ssq_ref)

    for rows, cols in _chunks(o_ref.shape):
        x_new = res_ref[rows, cols] + _dot(a_ref[rows, :], b_ref[:, cols])
        o_ref[rows, cols] = x_new
        if emit_norm:
            xb_ref[rows, cols] = x_new.astype(xb_ref.dtype)
            ssq_ref[rows, :] += _lane_partial_ssq(x_new)


def _mm_residual(a, b, res, k_start=0, k_size=None, emit_norm=False, pending=()):
    t, k = a.shape
    n = b.shape[1]
    k_size = k if k_size is None else k_size
    tm, tn = (2048, 256) if 2 * 2048 * k_size * 2 <= V7X_VMEM_BYTES // 2 else (1024, 512)
    tm, tn = _tile(t, tm), _tile(n, tn)
    ni, nj = t // tm, n // tn
    tile = pl.BlockSpec((tm, tn), lambda i, j: (i, j))
    out_specs = [tile]
    out_shape = [jax.ShapeDtypeStruct((t, n), F32)]
    if emit_norm:
        out_specs += [tile, pl.BlockSpec((tm, LANES), lambda i, j: (i, 0))]
        out_shape += [jax.ShapeDtypeStruct((t, n), BF16), jax.ShapeDtypeStruct((t, LANES), F32)]
    r_ops, r_in, r_out, r_shape, scaled = _attach_riders(pending, ni, nj)
    out = pl.pallas_call(
        functools.partial(_mm_residual_kernel, emit_norm=emit_norm, rider_scaled=scaled),
        grid=(ni, nj),
        in_specs=[pl.BlockSpec((pl.Element(tm), pl.Element(k_size)), lambda i, j: (i * tm, k_start)),
                  pl.BlockSpec((pl.Element(k_size), pl.Element(tn)), lambda i, j: (k_start, j * tn)),
                  tile] + r_in,
        out_specs=out_specs + r_out,
        out_shape=out_shape + r_shape,
        compiler_params=_params("parallel", "arbitrary"),
        name="mm_residual_norm" if emit_norm else "mm_residual",
    )(a, b, res, *r_ops)
    n_out = len(out_shape)
    return (*out[:n_out], out[n_out:])


def _sgu_kernel(u_ref, v_ref, g_ref, b_ref, ws_ref, bias_ref, o_ref, vn_ref, *,
                chunk, heads):
    v = v_ref[...]
    mu = jnp.mean(v, axis=-1, keepdims=True)
    vc = v - mu
    r = lax.rsqrt(jnp.mean(vc * vc, axis=-1, keepdims=True) + EPS)
    vn_ref[...] = (vc * r * g_ref[...] + b_ref[...]).astype(vn_ref.dtype)
    hd = v.shape[1] // heads
    for c in range(v.shape[0] // chunk):
        rows = pl.ds(c * chunk, chunk)
        for h in range(heads):
            cols = pl.ds(h * hd, hd)
            s = _dot(ws_ref[h], vn_ref[rows, cols]) + bias_ref[:, cols]
            o_ref[rows, cols] = (u_ref[rows, cols].astype(F32) * s).astype(o_ref.dtype)


def _sgu(u, v, ln_g, ln_b, w_s, b_s, tm=512):
    t, d = v.shape
    heads, chunk, _ = w_s.shape
    bias = jnp.repeat(jnp.transpose(b_s), d // heads, axis=1)
    row = lambda i: (i, 0)
    fixed2 = lambda i: (0, 0)
    return pl.pallas_call(
        functools.partial(_sgu_kernel, chunk=chunk, heads=heads),
        grid=(t // tm,),
        in_specs=[pl.BlockSpec((tm, d), row),
                  pl.BlockSpec((tm, d), row),
                  pl.BlockSpec((1, d), fixed2),
                  pl.BlockSpec((1, d), fixed2),
                  pl.BlockSpec((heads, chunk, chunk), lambda i: (0, 0, 0)),
                  pl.BlockSpec((chunk, d), fixed2)],
        out_specs=pl.BlockSpec((tm, d), row),
        out_shape=jax.ShapeDtypeStruct((t, d), BF16),
        scratch_shapes=[pltpu.VMEM((tm, d), BF16)],
        compiler_params=_params("parallel"),
        name="sgu",
    )(u, v, ln_g.reshape(1, d), ln_b.reshape(1, d), w_s.astype(BF16), bias)


def _dft_tables(n, cols=None, residues=1, split=64):
    cols = n if cols is None else cols
    split = min(split, n)
    j = jnp.arange(cols, dtype=jnp.int32)[None, :]
    unit = 2.0 * math.pi / n
    a_lo = ((jnp.arange(split, dtype=jnp.int32)[:, None] * j) % n).astype(F32) * unit
    a_hi = (((jnp.arange(n // split, dtype=jnp.int32) * split)[:, None] * j) % n).astype(F32) * unit
    scale = 1.0 / math.sqrt(n)

    def lo(t):
        return jnp.transpose(t.reshape(split // residues, residues, cols), (1, 0, 2))[:, None]

    c_lo, s_lo = lo(jnp.cos(a_lo)), lo(jnp.sin(a_lo))
    c_hi, s_hi = (jnp.cos(a_hi) * scale)[None, :, None, :], (jnp.sin(a_hi) * scale)[None, :, None, :]
    cos = (c_hi * c_lo - s_hi * s_lo).reshape(residues, n // residues, cols)
    sin = (s_hi * c_lo + c_hi * s_lo).reshape(residues, n // residues, cols)
    return cos, sin


def _position_tables(seq):
    cos, sin = _dft_tables(seq, cols=seq // DFT_RADIX, residues=DFT_RADIX)
    return jnp.concatenate([cos, -sin], axis=2).astype(BF16)


def _dft_channels_kernel(h_ref, ssq_ref, cs_ref, pq_ref, r_ref):
    radix, tm, d = h_ref.shape
    groups, gd, _ = cs_ref.shape
    width = min(V7X_MXU_DIM, gd)
    for b in range(radix):
        r_ref[b] = _row_rsqrt(ssq_ref[b], d)
    granules = tm // 2 // BF16_ROWS
    for t0 in (0, tm // 2):
        for g in range(groups):
            gcols = slice(g * gd, (g + 1) * gd)
            lhs = jnp.concatenate(
                [h_ref[b, t0 + u * BF16_ROWS:t0 + (u + 1) * BF16_ROWS, gcols]
                 for u in range(granules) for b in range(radix)], axis=0)
            for si in range(gd // width):
                acc = _dot(lhs, cs_ref[g, :, 2 * si * width:2 * (si + 1) * width])
                cols = slice(g * gd + si * width, g * gd + (si + 1) * width)
                for u in range(granules):
                    rows = slice(t0 + u * BF16_ROWS, t0 + (u + 1) * BF16_ROWS)
                    p, q = [], []
                    for b in range(radix):
                        lo = (u * radix + b) * BF16_ROWS
                        blk = _scale_rows(acc[lo:lo + BF16_ROWS], r_ref[b, rows, :])
                        p.append(blk[:, :width])
                        q.append(blk[:, width:])
                    pa, pb, pc, pd = p[0] + p[2], p[1] + p[3], p[0] - p[2], p[1] - p[3]
                    qa, qb, qc, qd = q[0] + q[2], q[1] + q[3], q[0] - q[2], q[1] - q[3]
                    out = [(pa + pb, qa + qb), (pc - qd, qc + pd), (pa - pb, qa - qb), (pc + qd, qc - pd)]
                    for k, (vp, vq) in enumerate(out):
                        pq_ref[k, 0, rows, cols] = vp.astype(pq_ref.dtype)
                        pq_ref[k, 1, rows, cols] = vq.astype(pq_ref.dtype)


def _dft_channels(h, ssq, cs, batch, tm=128):
    t, d = h.shape
    seq = t // batch
    quarter = seq // DFT_RADIX
    tm = _tile(quarter, tm)
    return pl.pallas_call(
        _dft_channels_kernel,
        grid=(batch, quarter // tm),
        in_specs=[pl.BlockSpec((None, DFT_RADIX, tm, d), lambda b, i: (b, 0, i, 0)),
                  pl.BlockSpec((None, DFT_RADIX, tm, LANES), lambda b, i: (b, 0, i, 0)),
                  pl.BlockSpec(cs.shape, lambda b, i: (0, 0, 0))],
        out_specs=pl.BlockSpec((None, DFT_RADIX, 2, tm, d), lambda b, i: (b, 0, 0, i, 0)),
        out_shape=jax.ShapeDtypeStruct((batch, DFT_RADIX, 2, quarter, d), BF16),
        scratch_shapes=[pltpu.VMEM((DFT_RADIX, tm, LANES), F32)],
        compiler_params=_params("parallel", "parallel"),
        name="dft_channels",
    )(h.reshape(batch, DFT_RADIX, quarter, d), ssq.reshape(batch, DFT_RADIX, quarter, LANES), cs)


def _dft_positions_kernel(trig_ref, pq_ref, o_ref, il_ref):
    radix, quarter, _ = trig_ref.shape
    for ci, cols in enumerate(_slices(o_ref.shape[1], V7X_MXU_DIM)):
        il = il_ref.at[ci % 2]
        lane_groups = _slices(cols.stop - cols.start, LANES)
        for k in range(radix):
            for rows in _slices(quarter, 2 * V7X_MXU_DIM):
                acc = _dot(trig_ref[k, rows, :], pq_ref[k, :, cols])
                dst = pl.ds(radix * rows.start + k, rows.stop - rows.start, stride=radix)
                for c, lanes in enumerate(lane_groups):
                    il[c, dst, :] = acc[:, lanes]
        for c, lanes in enumerate(lane_groups):
            o_ref[:, cols.start + lanes.start:cols.start + lanes.stop] = il[c].astype(o_ref.dtype)


def _dft_positions(trig, pq):
    batch, radix, k2, d = pq.shape
    seq = radix * k2 // 2
    tn = _tile(d, (8 * 1024 * 1024) // (2 * radix * k2))
    return pl.pallas_call(
        _dft_positions_kernel,
        grid=(batch, d // tn),
        in_specs=[pl.BlockSpec(trig.shape, lambda b, j: (0, 0, 0), pipeline_mode=pl.Buffered(1)),
                  pl.BlockSpec((None, radix, k2, tn), lambda b, j: (b, 0, 0, j))],
        out_specs=pl.BlockSpec((None, seq, tn), lambda b, j: (b, 0, j)),
        out_shape=jax.ShapeDtypeStruct((batch, seq, d), BF16),
        scratch_shapes=[pltpu.VMEM((2, V7X_MXU_DIM // LANES, seq, LANES), F32)],
        compiler_params=_params("parallel", "arbitrary"),
        name="dft_positions",
    )(trig, pq)


def _resolve(weights, name):
    if isinstance(weights[name], _Pending):
        weights[name] = _prep_weight(*weights[name])
    return weights[name]


def _take_pending(weights, names):
    names = [n for n in names if weights is not None and isinstance(weights.get(n), _Pending)]
    return names, [weights[n] for n in names]


def _ffn(x, xb, ssq, layer, next_layer):
    w_gate, w_up, w_down = (_resolve(layer, n) for n in ("w_gate", "w_up", "w_down"))
    names, pending = _take_pending(next_layer, ("w_gate", "w_up", "w_down", "w_out"))
    hidden, prepared = _mm_swiglu(xb, ssq, w_gate, w_up, pending)
    if names:
        next_layer.update(zip(names, prepared))
    k = w_down.shape[0]
    k_lo = _ceil_div(_ceil_div(k, V7X_MXU_DIM), 2) * V7X_MXU_DIM
    x_mid, _ = _mm_residual(hidden, w_down, x, k_start=0, k_size=k_lo)
    x, xb, ssq, _ = _mm_residual(hidden, w_down, x_mid, k_start=k_lo, k_size=k - k_lo, emit_norm=True)
    return x, xb, ssq


def _gmlp_layer(x, xb, ssq, layer):
    w_in = _resolve(layer, "w_in")
    d_u = layer["w_out"].w3.shape[1] if isinstance(layer["w_out"], _Pending) else layer["w_out"].shape[0]
    tn = _tile(d_u, 1024)
    r = _row_scale(ssq, x.shape[1])
    names, pending = _take_pending(layer, ("w_gate", "w_out"))
    u, prepared = _mm_gelu(xb, r, w_in, 0, d_u, BF16, pending, tn=tn)
    layer.update(zip(names, prepared))
    names, pending = _take_pending(layer, ("w_up",))
    v, prepared = _mm_gelu(xb, r, w_in, d_u // tn, d_u, F32, pending, tn=tn)
    layer.update(zip(names, prepared))
    gated = _sgu(u, v, layer["ln_g"], layer["ln_b"], layer["w_s"], layer["b_s"])
    names, pending = _take_pending(layer, ("w_down",))
    x, xb, ssq, prepared = _mm_residual(gated, layer["w_out"], x, emit_norm=True, pending=pending)
    layer.update(zip(names, prepared))
    return x, xb, ssq


def _fourier_layer(x, xb, ssq, batch, layer):
    t, d = x.shape
    seq = t // batch
    quarter = seq // DFT_RADIX
    cs = _resolve(layer, "cs").reshape(B_GROUPS, d // B_GROUPS, 2 * d // B_GROUPS)
    pq = _dft_channels(xb, ssq, cs, batch)
    y = _dft_positions(_position_tables(seq), pq.reshape(batch, DFT_RADIX, 2 * quarter, d))
    x, xb, ssq, _ = _mm_residual(y.reshape(t, d), _resolve(layer, "w_out"), x, emit_norm=True)
    return x, xb, ssq


def _trunk(x3, layers, final_norm_g):
    batch, seq, d = x3.shape
    x = x3.reshape(batch * seq, d)
    xb, ssq = _cast_ssq(x)
    for i, layer in enumerate(layers):
        if i % 2 == 0:
            x, xb, ssq = _gmlp_layer(x, xb, ssq, layer)
        else:
            x, xb, ssq = _fourier_layer(x, xb, ssq, batch, layer)
        x, xb, ssq = _ffn(x, xb, ssq, layer, layers[i + 1] if i + 1 < len(layers) else None)
    return _final_norm(x, ssq, final_norm_g).reshape(batch, seq, d)


def kernel(x_prompt, x_sample, a_norm_g, a_w_in, a_ln_g, a_ln_b, a_w_s, a_b_s, a_w_out,
           b_norm_g, b_w_out, ffn_norm_g, ffn_w_gate, ffn_w_up, ffn_w_down, final_norm_g):
    depth, d, _ = ffn_w_gate.shape
    gd = d // B_GROUPS
    width = min(V7X_MXU_DIM, gd)
    cs_pair = jnp.stack([t.reshape(gd, gd // width, width) for t in _dft_tables(gd)], axis=2)
    cs3 = jnp.tile(cs_pair.reshape(gd, 2 * gd), (B_GROUPS, 1))[None]
    layers = []
    for i in range(depth):
        j = i // 2
        layer = dict(w_gate=_Pending(ffn_w_gate, i, ffn_norm_g[i]),
                     w_up=_Pending(ffn_w_up, i, ffn_norm_g[i]),
                     w_down=_Pending(ffn_w_down, i, None))
        if i % 2 == 0:
            layer.update(w_in=_Pending(a_w_in, j, a_norm_g[j]), w_out=_Pending(a_w_out, j, None),
                         ln_g=a_ln_g[j], ln_b=a_ln_b[j], w_s=a_w_s[j], b_s=a_b_s[j])
        else:
            layer.update(cs=_Pending(cs3, 0, b_norm_g[j]), w_out=_Pending(b_w_out, j, None))
        layers.append(layer)
    return (_trunk(x_prompt, layers, final_norm_g), _trunk(x_sample, layers, final_norm_g))
```

```python
import functools
import math
from typing import NamedTuple

import jax
import jax.numpy as jnp
from jax import lax
from jax.experimental import pallas as pl
from jax.experimental.pallas import tpu as pltpu

EPS = 1e-6
B_GROUPS = 8
DFT_RADIX = 4

V7X_VMEM_BYTES = 64 * 1024 * 1024
VMEM_LIMIT_BYTES = V7X_VMEM_BYTES - 8 * 1024 * 1024
LANES = 128
BF16_ROWS = 16
V7X_MXU_DIM = 256

F32 = jnp.float32
BF16 = jnp.bfloat16


def _params(*semantics, flags=None):
    return pltpu.CompilerParams(dimension_semantics=semantics,
                                vmem_limit_bytes=VMEM_LIMIT_BYTES, flags=flags)


def _dot(a, b):
    return jnp.dot(a, b, preferred_element_type=F32)


def _ceil_div(a, b):
    return -(-a // b)


def _tile(total, preferred):
    t = min(total, preferred)
    assert total % t == 0, (total, preferred)
    return t


def _slices(total, width):
    width = min(width, total)
    assert total % width == 0, (total, width)
    return [slice(c * width, (c + 1) * width) for c in range(total // width)]


def _chunks(shape, rows=2 * V7X_MXU_DIM, cols=V7X_MXU_DIM, tail_rows=None):
    col_slices = _slices(shape[1], cols)
    out = [(r, c) for c in col_slices[:-1] for r in _slices(shape[0], rows)]
    return out + [(r, col_slices[-1]) for r in _slices(shape[0], tail_rows or rows)]


def _scale_rows(acc, r):
    return acc * jnp.tile(r, (1, acc.shape[1] // LANES))


def _row_rsqrt(ssq_lanes, d_model):
    tot = jnp.sum(ssq_lanes, axis=1, keepdims=True)
    return jnp.broadcast_to(lax.rsqrt(tot / d_model + EPS), ssq_lanes.shape)


def _lane_partial_ssq(x):
    sq = x * x
    part = sq[:, :LANES]
    for c in range(1, x.shape[1] // LANES):
        part = part + sq[:, c * LANES:(c + 1) * LANES]
    return part


def _prep_scaled_kernel(w_ref, g_ref, o_ref):
    o_ref[...] = (w_ref[...] * g_ref[...]).astype(o_ref.dtype)


def _prep_plain_kernel(w_ref, o_ref):
    o_ref[...] = w_ref[...].astype(o_ref.dtype)


def _prep_weight(w3, layer, g=None, tk=256):
    _, k, n = w3.shape
    tn = n
    while tn > 5504:
        tn //= 2
    assert n % tn == 0 and tn % LANES == 0 and k % tk == 0
    w_spec = pl.BlockSpec((None, tk, tn), lambda i, j: (layer, i, j))
    o_spec = pl.BlockSpec((tk, tn), lambda i, j: (i, j))
    common = dict(grid=(k // tk, n // tn), out_specs=o_spec,
                  out_shape=jax.ShapeDtypeStruct((k, n), BF16),
                  compiler_params=_params("parallel", "parallel"))
    if g is None:
        return pl.pallas_call(_prep_plain_kernel, in_specs=[w_spec], name="prep_plain", **common)(w3)
    g_spec = pl.BlockSpec((tk, 1), lambda i, j: (i, 0))
    return pl.pallas_call(_prep_scaled_kernel, in_specs=[w_spec, g_spec], name="prep_scaled",
                          **common)(w3, g.reshape(k, 1))


def _cast_ssq_kernel(x_ref, xb_ref, ssq_ref):
    x = x_ref[...]
    xb_ref[...] = x.astype(xb_ref.dtype)
    ssq_ref[...] = _lane_partial_ssq(x)


def _cast_ssq(x, tm=512):
    t, d = x.shape
    row = lambda i: (i, 0)
    return pl.pallas_call(
        _cast_ssq_kernel,
        grid=(t // tm,),
        in_specs=[pl.BlockSpec((tm, d), row)],
        out_specs=[pl.BlockSpec((tm, d), row), pl.BlockSpec((tm, LANES), row)],
        out_shape=[jax.ShapeDtypeStruct((t, d), BF16), jax.ShapeDtypeStruct((t, LANES), F32)],
        compiler_params=_params("parallel"),
        name="cast_ssq",
    )(x)


def _final_norm_kernel(x_ref, ssq_ref, g_ref, o_ref):
    x = x_ref[...]
    o_ref[...] = _scale_rows(x, _row_rsqrt(ssq_ref[...], x.shape[1])) * g_ref[...]


def _final_norm(x, ssq, g, tm=512):
    t, d = x.shape
    row = lambda i: (i, 0)
    return pl.pallas_call(
        _final_norm_kernel,
        grid=(t // tm,),
        in_specs=[pl.BlockSpec((tm, d), row), pl.BlockSpec((tm, LANES), row),
                  pl.BlockSpec((1, d), lambda i: (0, 0))],
        out_specs=pl.BlockSpec((tm, d), row),
        out_shape=jax.ShapeDtypeStruct((t, d), F32),
        compiler_params=_params("parallel"),
        name="final_norm",
    )(x, ssq, g.reshape(1, d))


def _row_scale_kernel(ssq_ref, r_ref, *, d_model):
    r_ref[...] = _row_rsqrt(ssq_ref[...], d_model)


def _row_scale(ssq, d_model, tm=2048):
    t = ssq.shape[0]
    tm = _tile(t, tm)
    spec = pl.BlockSpec((tm, LANES), lambda i: (i, 0))
    return pl.pallas_call(
        functools.partial(_row_scale_kernel, d_model=d_model),
        grid=(t // tm,), in_specs=[spec], out_specs=spec,
        out_shape=jax.ShapeDtypeStruct(ssq.shape, F32),
        compiler_params=_params("parallel"),
        name="row_scale",
    )(ssq)


class _Pending(NamedTuple):
    w3: jax.Array
    layer: int
    g: jax.Array | None


def _rider_block(rows, cols, ni, nj):
    def split(size, parts, align):
        blk = _ceil_div(_ceil_div(size, parts), align) * align
        return blk, _ceil_div(size, blk)

    options = []
    for swap in (False, True):
        parts_r, parts_c = (nj, ni) if swap else (ni, nj)
        (bk, nbk), (bn, nbn) = split(rows, parts_r, BF16_ROWS), split(cols, parts_c, LANES)
        options.append((bk * bn, swap, bk, bn, nbk, nbn))
    _, swap, bk, bn, nbk, nbn = min(options)

    def index_map(i, j):
        ri, ci = (j, i) if swap else (i, j)
        return jnp.minimum(ri, nbk - 1), jnp.minimum(ci, nbn - 1)

    return (bk, bn), index_map


def _attach_riders(pending, ni, nj):
    operands, in_specs, out_specs, out_shape, scaled = [], [], [], [], []
    for p in pending:
        (bk, bn), index_map = _rider_block(*p.w3.shape[1:], ni, nj)
        operands.append(p.w3)
        in_specs.append(pl.BlockSpec((None, bk, bn),
                                     lambda i, j, p=p, index_map=index_map: (p.layer, *index_map(i, j))))
        if p.g is not None:
            operands.append(p.g.reshape(-1, 1))
            in_specs.append(pl.BlockSpec((bk, 1), lambda i, j, index_map=index_map: (index_map(i, j)[0], 0)))
        scaled.append(p.g is not None)
        out_specs.append(pl.BlockSpec((bk, bn), index_map))
        out_shape.append(jax.ShapeDtypeStruct(p.w3.shape[1:], BF16))
    return operands, in_specs, out_specs, out_shape, tuple(scaled)


def _run_riders(in_refs, out_refs, scaled):
    in_refs = iter(in_refs)
    for o_ref, has_g in zip(out_refs, scaled):
        w = next(in_refs)[...]
        if has_g:
            w = w * next(in_refs)[...]
        o_ref[...] = w.astype(o_ref.dtype)


def _n_rider_inputs(rider_scaled):
    return len(rider_scaled) + sum(rider_scaled)


def _mm_gelu_kernel(a_ref, r_ref, b_ref, *refs, rider_scaled):
    n_in = _n_rider_inputs(rider_scaled)
    rider_in, o_ref, rider_out = refs[:n_in], refs[n_in], refs[n_in + 1:]
    _run_riders(rider_in, rider_out, rider_scaled)
    r = r_ref[...]
    for cols in _slices(o_ref.shape[1], V7X_MXU_DIM):
        acc = _scale_rows(_dot(a_ref[...], b_ref[:, cols]), r)
        o_ref[:, cols] = jax.nn.gelu(acc).astype(o_ref.dtype)


def _mm_gelu(a, r, b, col_block_offset, n, out_dtype, pending=(), tm=1024, tn=1024):
    t, k = a.shape
    tm, tn = _tile(t, tm), _tile(n, tn)
    ni, nj = t // tm, n // tn
    r_ops, r_in, r_out, r_shape, scaled = _attach_riders(pending, ni, nj)
    out, *prepared = pl.pallas_call(
        functools.partial(_mm_gelu_kernel, rider_scaled=scaled),
        grid=(ni, nj),
        in_specs=[pl.BlockSpec((tm, k), lambda i, j: (i, 0)),
                  pl.BlockSpec((tm, LANES), lambda i, j: (i, 0)),
                  pl.BlockSpec((k, tn), lambda i, j: (0, j + col_block_offset))] + r_in,
        out_specs=[pl.BlockSpec((tm, tn), lambda i, j: (i, j))] + r_out,
        out_shape=[jax.ShapeDtypeStruct((t, n), out_dtype)] + r_shape,
        compiler_params=_params("parallel", "arbitrary"),
        name="mm_gelu",
    )(a, r, b, *r_ops)
    return out, prepared


_ROW_JOBS = {"cast": (_cast_ssq_kernel, 1, 2), "norm": (_final_norm_kernel, 3, 1)}


def _attach_row_job(job, ni, nj):
    if job is None:
        return [], [], [], [], None
    kind, x, *rest = job
    t, d = x.shape
    rb = _ceil_div(_ceil_div(t, ni * nj), BF16_ROWS) * BF16_ROWS
    last = _ceil_div(t, rb) - 1
    row = lambda i, j: (jnp.minimum(i * nj + j, last), 0)
    wide, lanes = pl.BlockSpec((rb, d), row), pl.BlockSpec((rb, LANES), row)
    if kind == "cast":
        return ([x], [wide], [wide, lanes],
                [jax.ShapeDtypeStruct((t, d), BF16), jax.ShapeDtypeStruct((t, LANES), F32)], kind)
    ssq, g = rest
    return ([x, ssq, g.reshape(1, d)], [wide, lanes, pl.BlockSpec((1, d), lambda i, j: (0, 0))],
            [wide], [jax.ShapeDtypeStruct((t, d), F32)], kind)


def _mm_swiglu_kernel(a_ref, ssq_ref, wg_ref, wu_ref, *refs, rider_scaled, row_job):
    n_rider_in = _n_rider_inputs(rider_scaled)
    body, n_job_in, n_job_out = _ROW_JOBS[row_job] if row_job else (None, 0, 0)
    rider_in, refs = refs[:n_rider_in], refs[n_rider_in:]
    job_in, refs = refs[:n_job_in], refs[n_job_in:]
    o_ref, *outs, r_ref = refs
    rider_out, job_out = outs[:len(rider_scaled)], outs[len(rider_scaled):]
    assert len(job_out) == n_job_out
    _run_riders(rider_in, rider_out, rider_scaled)
    if body is not None:
        body(*job_in, *job_out)
    r_ref[...] = _row_rsqrt(ssq_ref[...], a_ref.shape[1])
    for rows, cols in _chunks(o_ref.shape, rows=V7X_MXU_DIM):
        a = a_ref[rows, :]
        r = r_ref[rows, :]
        g = _scale_rows(_dot(a, wg_ref[:, cols]), r)
        u = _scale_rows(_dot(a, wu_ref[:, cols]), r)
        o_ref[rows, cols] = (g * jax.nn.sigmoid(g) * u).astype(o_ref.dtype)


def _mm_swiglu(a, ssq, wg, wu, pending=(), row_job=None, tm=2048, tn=V7X_MXU_DIM):
    t, k = a.shape
    n = wg.shape[1]
    tm, tn = _tile(t, tm), _tile(n, tn)
    ni, nj = t // tm, n // tn
    w_spec = pl.BlockSpec((k, tn), lambda i, j: (0, j))
    r_ops, r_in, r_out, r_shape, scaled = _attach_riders(pending, ni, nj)
    j_ops, j_in, j_out, j_shape, job_kind = _attach_row_job(row_job, ni, nj)
    hidden, *extra = pl.pallas_call(
        functools.partial(_mm_swiglu_kernel, rider_scaled=scaled, row_job=job_kind),
        grid=(ni, nj),
        in_specs=[pl.BlockSpec((tm, k), lambda i, j: (i, 0)),
                  pl.BlockSpec((tm, LANES), lambda i, j: (i, 0)), w_spec, w_spec] + r_in + j_in,
        out_specs=[pl.BlockSpec((tm, tn), lambda i, j: (i, j))] + r_out + j_out,
        out_shape=[jax.ShapeDtypeStruct((t, n), BF16)] + r_shape + j_shape,
        scratch_shapes=[pltpu.VMEM((tm, LANES), F32)],
        compiler_params=_params("arbitrary" if row_job else "parallel", "arbitrary"),
        name="mm_swiglu",
    )(a, ssq, wg, wu, *r_ops, *j_ops)
    return hidden, extra[:len(pending)], extra[len(pending):]


def _mm_residual_kernel(a_ref, b_ref, res_ref, *refs, emit_norm, rider_scaled):
    n_in = _n_rider_inputs(rider_scaled)
    n_out = 3 if emit_norm else 1
    rider_in, o_ref, rider_out = refs[:n_in], refs[n_in], refs[n_in + n_out:]
    _run_riders(rider_in, rider_out, rider_scaled)
    if emit_norm:
        xb_ref, ssq_ref = refs[n_in + 1:n_in + 3]

        @pl.when(pl.program_id(1) == 0)
        def _():
            ssq_ref[...] = jnp.zeros_like(ssq_ref)

    for rows, cols in _chunks(o_ref.shape):
        x_new = res_ref[rows, cols] + _dot(a_ref[rows, :], b_ref[:, cols])
        o_ref[rows, cols] = x_new
        if emit_norm:
            xb_ref[rows, cols] = x_new.astype(xb_ref.dtype)
            ssq_ref[rows, :] += _lane_partial_ssq(x_new)


def _mm_residual(a, b, res, k_start=0, k_size=None, emit_norm=False, pending=()):
    t, k = a.shape
    n = b.shape[1]
    k_size = k if k_size is None else k_size
    tm, tn = (2048, 256) if 2 * 2048 * k_size * 2 <= V7X_VMEM_BYTES // 2 else (1024, 512)
    tm, tn = _tile(t, tm), _tile(n, tn)
    ni, nj = t // tm, n // tn
    tile = pl.BlockSpec((tm, tn), lambda i, j: (i, j))
    out_specs = [tile]
    out_shape = [jax.ShapeDtypeStruct((t, n), F32)]
    if emit_norm:
        out_specs += [tile, pl.BlockSpec((tm, LANES), lambda i, j: (i, 0))]
        out_shape += [jax.ShapeDtypeStruct((t, n), BF16), jax.ShapeDtypeStruct((t, LANES), F32)]
    r_ops, r_in, r_out, r_shape, scaled = _attach_riders(pending, ni, nj)
    out = pl.pallas_call(
        functools.partial(_mm_residual_kernel, emit_norm=emit_norm, rider_scaled=scaled),
        grid=(ni, nj),
        in_specs=[pl.BlockSpec((pl.Element(tm), pl.Element(k_size)), lambda i, j: (i * tm, k_start)),
                  pl.BlockSpec((pl.Element(k_size), pl.Element(tn)), lambda i, j: (k_start, j * tn)),
                  tile] + r_in,
        out_specs=out_specs + r_out,
        out_shape=out_shape + r_shape,
        compiler_params=_params("parallel", "arbitrary"),
        name="mm_residual_norm" if emit_norm else "mm_residual",
    )(a, b, res, *r_ops)
    n_out = len(out_shape)
    return (*out[:n_out], out[n_out:])


def _sgu_kernel(u_ref, v_ref, g_ref, b_ref, ws_ref, bias_ref, o_ref, vn_ref, *,
                chunk, heads):
    v = v_ref[...]
    mu = jnp.mean(v, axis=-1, keepdims=True)
    vc = v - mu
    r = lax.rsqrt(jnp.mean(vc * vc, axis=-1, keepdims=True) + EPS)
    vn_ref[...] = (vc * r * g_ref[...] + b_ref[...]).astype(vn_ref.dtype)
    hd = v.shape[1] // heads
    for c in range(v.shape[0] // chunk):
        rows = pl.ds(c * chunk, chunk)
        for h in range(heads):
            cols = pl.ds(h * hd, hd)
            s = _dot(ws_ref[h], vn_ref[rows, cols]) + bias_ref[:, cols]
            o_ref[rows, cols] = (u_ref[rows, cols].astype(F32) * s).astype(o_ref.dtype)


def _sgu(u, v, ln_g, ln_b, w_s, b_s, tm=512):
    t, d = v.shape
    heads, chunk, _ = w_s.shape
    bias = jnp.repeat(jnp.transpose(b_s), d // heads, axis=1)
    row = lambda i: (i, 0)
    fixed2 = lambda i: (0, 0)
    return pl.pallas_call(
        functools.partial(_sgu_kernel, chunk=chunk, heads=heads),
        grid=(t // tm,),
        in_specs=[pl.BlockSpec((tm, d), row),
                  pl.BlockSpec((tm, d), row),
                  pl.BlockSpec((1, d), fixed2),
                  pl.BlockSpec((1, d), fixed2),
                  pl.BlockSpec((heads, chunk, chunk), lambda i: (0, 0, 0)),
                  pl.BlockSpec((chunk, d), fixed2)],
        out_specs=pl.BlockSpec((tm, d), row),
        out_shape=jax.ShapeDtypeStruct((t, d), BF16),
        scratch_shapes=[pltpu.VMEM((tm, d), BF16)],
        compiler_params=_params("parallel"),
        name="sgu",
    )(u, v, ln_g.reshape(1, d), ln_b.reshape(1, d), w_s.astype(BF16), bias)


def _dft_tables(n, cols=None, residues=1, split=64):
    cols = n if cols is None else cols
    split = min(split, n)
    j = jnp.arange(cols, dtype=jnp.int32)[None, :]
    unit = 2.0 * math.pi / n
    a_lo = ((jnp.arange(split, dtype=jnp.int32)[:, None] * j) % n).astype(F32) * unit
    a_hi = (((jnp.arange(n // split, dtype=jnp.int32) * split)[:, None] * j) % n).astype(F32) * unit
    scale = 1.0 / math.sqrt(n)

    def lo(t):
        return jnp.transpose(t.reshape(split // residues, residues, cols), (1, 0, 2))[:, None]

    c_lo, s_lo = lo(jnp.cos(a_lo)), lo(jnp.sin(a_lo))
    c_hi, s_hi = (jnp.cos(a_hi) * scale)[None, :, None, :], (jnp.sin(a_hi) * scale)[None, :, None, :]
    cos = (c_hi * c_lo - s_hi * s_lo).reshape(residues, n // residues, cols)
    sin = (s_hi * c_lo + c_hi * s_lo).reshape(residues, n // residues, cols)
    return cos, sin


def _position_tables(seq):
    cos, sin = _dft_tables(seq, cols=seq // DFT_RADIX, residues=DFT_RADIX)
    return jnp.concatenate([cos, -sin], axis=2).astype(BF16)


def _dft_channels_kernel(h_ref, ssq_ref, cs_ref, pq_ref, r_ref):
    radix, tm, d = h_ref.shape
    groups, gd, _ = cs_ref.shape
    width = min(V7X_MXU_DIM, gd)
    for b in range(radix):
        r_ref[b] = _row_rsqrt(ssq_ref[b], d)
    granules = tm // 2 // BF16_ROWS
    for t0 in (0, tm // 2):
        for g in range(groups):
            gcols = slice(g * gd, (g + 1) * gd)
            lhs = jnp.concatenate(
                [h_ref[b, t0 + u * BF16_ROWS:t0 + (u + 1) * BF16_ROWS, gcols]
                 for u in range(granules) for b in range(radix)], axis=0)
            for si in range(gd // width):
                acc = _dot(lhs, cs_ref[g, :, 2 * si * width:2 * (si + 1) * width])
                cols = slice(g * gd + si * width, g * gd + (si + 1) * width)
                for u in range(granules):
                    rows = slice(t0 + u * BF16_ROWS, t0 + (u + 1) * BF16_ROWS)
                    p, q = [], []
                    for b in range(radix):
                        lo = (u * radix + b) * BF16_ROWS
                        blk = _scale_rows(acc[lo:lo + BF16_ROWS], r_ref[b, rows, :])
                        p.append(blk[:, :width])
                        q.append(blk[:, width:])
                    pa, pb, pc, pd = p[0] + p[2], p[1] + p[3], p[0] - p[2], p[1] - p[3]
                    qa, qb, qc, qd = q[0] + q[2], q[1] + q[3], q[0] - q[2], q[1] - q[3]
                    out = [(pa + pb, qa + qb), (pc - qd, qc + pd), (pa - pb, qa - qb), (pc + qd, qc - pd)]
                    for k, (vp, vq) in enumerate(out):
                        pq_ref[k, 0, rows, cols] = vp.astype(pq_ref.dtype)
                        pq_ref[k, 1, rows, cols] = vq.astype(pq_ref.dtype)


def _dft_channels(h, ssq, cs, batch, tm=128):
    t, d = h.shape
    seq = t // batch
    quarter = seq // DFT_RADIX
    tm = _tile(quarter, tm)
    return pl.pallas_call(
        _dft_channels_kernel,
        grid=(batch, quarter // tm),
        in_specs=[pl.BlockSpec((None, DFT_RADIX, tm, d), lambda b, i: (b, 0, i, 0)),
                  pl.BlockSpec((None, DFT_RADIX, tm, LANES), lambda b, i: (b, 0, i, 0)),
                  pl.BlockSpec(cs.shape, lambda b, i: (0, 0, 0))],
        out_specs=pl.BlockSpec((None, DFT_RADIX, 2, tm, d), lambda b, i: (b, 0, 0, i, 0)),
        out_shape=jax.ShapeDtypeStruct((batch, DFT_RADIX, 2, quarter, d), BF16),
        scratch_shapes=[pltpu.VMEM((DFT_RADIX, tm, LANES), F32)],
        compiler_params=_params("parallel", "parallel"),
        name="dft_channels",
    )(h.reshape(batch, DFT_RADIX, quarter, d), ssq.reshape(batch, DFT_RADIX, quarter, LANES), cs)


def _dft_positions_kernel(trig_ref, pq_ref, o_ref, il_ref):
    radix, quarter, _ = trig_ref.shape
    for ci, cols in enumerate(_slices(o_ref.shape[1], V7X_MXU_DIM)):
        il = il_ref.at[ci % 2]
        lane_groups = _slices(cols.stop - cols.start, LANES)
        for k in range(radix):
            for rows in _slices(quarter, 2 * V7X_MXU_DIM):
                acc = _dot(trig_ref[k, rows, :], pq_ref[k, :, cols])
                dst = pl.ds(radix * rows.start + k, rows.stop - rows.start, stride=radix)
                for c, lanes in enumerate(lane_groups):
                    il[c, dst, :] = acc[:, lanes]
        for c, lanes in enumerate(lane_groups):
            o_ref[:, cols.start + lanes.start:cols.start + lanes.stop] = il[c].astype(o_ref.dtype)


def _dft_positions(trig, pq):
    batch, radix, k2, d = pq.shape
    seq = radix * k2 // 2
    tn = _tile(d, (8 * 1024 * 1024) // (2 * radix * k2))
    return pl.pallas_call(
        _dft_positions_kernel,
        grid=(batch, d // tn),
        in_specs=[pl.BlockSpec(trig.shape, lambda b, j: (0, 0, 0), pipeline_mode=pl.Buffered(1)),
                  pl.BlockSpec((None, radix, k2, tn), lambda b, j: (b, 0, 0, j))],
        out_specs=pl.BlockSpec((None, seq, tn), lambda b, j: (b, 0, j)),
        out_shape=jax.ShapeDtypeStruct((batch, seq, d), BF16),
        scratch_shapes=[pltpu.VMEM((2, V7X_MXU_DIM // LANES, seq, LANES), F32)],
        compiler_params=_params("parallel", "arbitrary"),
        name="dft_positions",
    )(trig, pq)


def _resolve(weights, name):
    if isinstance(weights[name], _Pending):
        weights[name] = _prep_weight(*weights[name])
    return weights[name]


def _take_pending(weights, names):
    names = [n for n in names if weights is not None and isinstance(weights.get(n), _Pending)]
    return names, [weights[n] for n in names]


def _ffn(x, xb, ssq, layer, next_layer, row_job=None):
    w_gate, w_up, w_down = (_resolve(layer, n) for n in ("w_gate", "w_up", "w_down"))
    names, pending = _take_pending(next_layer, ("w_gate", "w_up", "w_down", "w_out"))
    hidden, prepared, job_out = _mm_swiglu(xb, ssq, w_gate, w_up, pending, row_job)
    if names:
        next_layer.update(zip(names, prepared))
    k = w_down.shape[0]
    k_lo = _ceil_div(_ceil_div(k, V7X_MXU_DIM), 2) * V7X_MXU_DIM
    x_mid, _ = _mm_residual(hidden, w_down, x, k_start=0, k_size=k_lo)
    x, xb, ssq, _ = _mm_residual(hidden, w_down, x_mid, k_start=k_lo, k_size=k - k_lo, emit_norm=True)
    return x, xb, ssq, job_out


def _gmlp_layer(x, xb, ssq, layer):
    w_in = _resolve(layer, "w_in")
    d_u = layer["w_out"].w3.shape[1] if isinstance(layer["w_out"], _Pending) else layer["w_out"].shape[0]
    tn = _tile(d_u, 1024)
    r = _row_scale(ssq, x.shape[1])
    names, pending = _take_pending(layer, ("w_gate", "w_out"))
    u, prepared = _mm_gelu(xb, r, w_in, 0, d_u, BF16, pending, tn=tn)
    layer.update(zip(names, prepared))
    names, pending = _take_pending(layer, ("w_up",))
    v, prepared = _mm_gelu(xb, r, w_in, d_u // tn, d_u, F32, pending, tn=tn)
    layer.update(zip(names, prepared))
    gated = _sgu(u, v, layer["ln_g"], layer["ln_b"], layer["w_s"], layer["b_s"])
    names, pending = _take_pending(layer, ("w_down",))
    x, xb, ssq, prepared = _mm_residual(gated, layer["w_out"], x, emit_norm=True, pending=pending)
    layer.update(zip(names, prepared))
    return x, xb, ssq


def _fourier_layer(x, xb, ssq, batch, layer):
    t, d = x.shape
    seq = t // batch
    quarter = seq // DFT_RADIX
    cs = _resolve(layer, "cs").reshape(B_GROUPS, d // B_GROUPS, 2 * d // B_GROUPS)
    pq = _dft_channels(xb, ssq, cs, batch)
    y = _dft_positions(_position_tables(seq), pq.reshape(batch, DFT_RADIX, 2 * quarter, d))
    x, xb, ssq, _ = _mm_residual(y.reshape(t, d), _resolve(layer, "w_out"), x, emit_norm=True)
    return x, xb, ssq


def _trunk(x, xb, ssq, batch, layers, row_job):
    job_out = None
    for i, layer in enumerate(layers):
        if i % 2 == 0:
            x, xb, ssq = _gmlp_layer(x, xb, ssq, layer)
        else:
            x, xb, ssq = _fourier_layer(x, xb, ssq, batch, layer)
        next_layer = layers[i + 1] if i + 1 < len(layers) else None
        x, xb, ssq, out = _ffn(x, xb, ssq, layer, next_layer, row_job if i == 0 else None)
        job_out = out if i == 0 else job_out
    return x, ssq, job_out


def kernel(x_prompt, x_sample, a_norm_g, a_w_in, a_ln_g, a_ln_b, a_w_s, a_b_s, a_w_out,
           b_norm_g, b_w_out, ffn_norm_g, ffn_w_gate, ffn_w_up, ffn_w_down, final_norm_g):
    depth, d, _ = ffn_w_gate.shape
    gd = d // B_GROUPS
    width = min(V7X_MXU_DIM, gd)
    cs_pair = jnp.stack([t.reshape(gd, gd // width, width) for t in _dft_tables(gd)], axis=2)
    cs3 = jnp.tile(cs_pair.reshape(gd, 2 * gd), (B_GROUPS, 1))[None]
    layers = []
    for i in range(depth):
        j = i // 2
        layer = dict(w_gate=_Pending(ffn_w_gate, i, ffn_norm_g[i]),
                     w_up=_Pending(ffn_w_up, i, ffn_norm_g[i]),
                     w_down=_Pending(ffn_w_down, i, None))
        if i % 2 == 0:
            layer.update(w_in=_Pending(a_w_in, j, a_norm_g[j]), w_out=_Pending(a_w_out, j, None),
                         ln_g=a_ln_g[j], ln_b=a_ln_b[j], w_s=a_w_s[j], b_s=a_b_s[j])
        else:
            layer.update(cs=_Pending(cs3, 0, b_norm_g[j]), w_out=_Pending(b_w_out, j, None))
        layers.append(layer)
    xp, xs = x_prompt.reshape(-1, d), x_sample.reshape(-1, d)
    xbp, ssqp = _cast_ssq(xp)
    xp, ssqp, (xbs, ssqs) = _trunk(xp, xbp, ssqp, x_prompt.shape[0], layers, ("cast", xs))
    xs, ssqs, (yp,) = _trunk(xs, xbs, ssqs, x_sample.shape[0], layers, ("norm", xp, ssqp, final_norm_g))
    ys = _final_norm(xs, ssqs, final_norm_g)
    return yp.reshape(x_prompt.shape), ys.reshape(x_sample.shape)
```

```python
import functools
import math
from typing import NamedTuple

import jax
import jax.numpy as jnp
from jax import lax
from jax.experimental import pallas as pl
from jax.experimental.pallas import tpu as pltpu

EPS = 1e-6
B_GROUPS = 8
DFT_RADIX = 4

V7X_VMEM_BYTES = 64 * 1024 * 1024
VMEM_LIMIT_BYTES = V7X_VMEM_BYTES - 8 * 1024 * 1024
LANES = 128
BF16_ROWS = 16
V7X_MXU_DIM = 256

F32 = jnp.float32
BF16 = jnp.bfloat16


def _params(*semantics):
    return pltpu.CompilerParams(dimension_semantics=semantics,
                                vmem_limit_bytes=VMEM_LIMIT_BYTES)


def _dot(a, b):
    return jnp.dot(a, b, preferred_element_type=F32)


def _ceil_div(a, b):
    return -(-a // b)


def _tile(total, preferred):
    t = min(total, preferred)
    assert total % t == 0, (total, preferred)
    return t


def _slices(total, width):
    width = min(width, total)
    assert total % width == 0, (total, width)
    return [slice(c * width, (c + 1) * width) for c in range(total // width)]


def _chunks(shape, rows=2 * V7X_MXU_DIM, cols=V7X_MXU_DIM):
    return [(r, c) for c in _slices(shape[1], cols) for r in _slices(shape[0], rows)]


def _scale_rows(acc, r):
    return acc * jnp.tile(r, (1, acc.shape[1] // LANES))


def _row_rsqrt(ssq_lanes, d_model):
    tot = jnp.sum(ssq_lanes, axis=1, keepdims=True)
    return jnp.broadcast_to(lax.rsqrt(tot / d_model + EPS), ssq_lanes.shape)


def _lane_partial_ssq(x):
    sq = x * x
    part = sq[:, :LANES]
    for c in range(1, x.shape[1] // LANES):
        part = part + sq[:, c * LANES:(c + 1) * LANES]
    return part


def _prep_scaled_kernel(w_ref, g_ref, o_ref):
    o_ref[...] = (w_ref[...] * g_ref[...]).astype(o_ref.dtype)


def _prep_plain_kernel(w_ref, o_ref):
    o_ref[...] = w_ref[...].astype(o_ref.dtype)


def _prep_weight(w3, layer, g=None, tk=256):
    _, k, n = w3.shape
    tn = n
    while tn > 5504:
        tn //= 2
    assert n % tn == 0 and tn % LANES == 0 and k % tk == 0
    w_spec = pl.BlockSpec((None, tk, tn), lambda i, j: (layer, i, j))
    o_spec = pl.BlockSpec((tk, tn), lambda i, j: (i, j))
    common = dict(grid=(k // tk, n // tn), out_specs=o_spec,
                  out_shape=jax.ShapeDtypeStruct((k, n), BF16),
                  compiler_params=_params("parallel", "parallel"))
    if g is None:
        return pl.pallas_call(_prep_plain_kernel, in_specs=[w_spec], name="prep_plain", **common)(w3)
    g_spec = pl.BlockSpec((tk, 1), lambda i, j: (i, 0))
    return pl.pallas_call(_prep_scaled_kernel, in_specs=[w_spec, g_spec], name="prep_scaled",
                          **common)(w3, g.reshape(k, 1))


def _cast_ssq_kernel(x_ref, xb_ref, ssq_ref):
    x = x_ref[...]
    xb_ref[...] = x.astype(xb_ref.dtype)
    ssq_ref[...] = _lane_partial_ssq(x)


def _cast_ssq(x, tm=512):
    t, d = x.shape
    row = lambda i: (i, 0)
    return pl.pallas_call(
        _cast_ssq_kernel,
        grid=(t // tm,),
        in_specs=[pl.BlockSpec((tm, d), row)],
        out_specs=[pl.BlockSpec((tm, d), row), pl.BlockSpec((tm, LANES), row)],
        out_shape=[jax.ShapeDtypeStruct((t, d), BF16), jax.ShapeDtypeStruct((t, LANES), F32)],
        compiler_params=_params("parallel"),
        name="cast_ssq",
    )(x)


def _final_norm_kernel(x_ref, ssq_ref, g_ref, o_ref):
    x = x_ref[...]
    o_ref[...] = _scale_rows(x, _row_rsqrt(ssq_ref[...], x.shape[1])) * g_ref[...]


def _final_norm(x, ssq, g, tm=512):
    t, d = x.shape
    row = lambda i: (i, 0)
    return pl.pallas_call(
        _final_norm_kernel,
        grid=(t // tm,),
        in_specs=[pl.BlockSpec((tm, d), row), pl.BlockSpec((tm, LANES), row),
                  pl.BlockSpec((1, d), lambda i: (0, 0))],
        out_specs=pl.BlockSpec((tm, d), row),
        out_shape=jax.ShapeDtypeStruct((t, d), F32),
        compiler_params=_params("parallel"),
        name="final_norm",
    )(x, ssq, g.reshape(1, d))


def _row_scale_kernel(ssq_ref, r_ref, *, d_model):
    r_ref[...] = _row_rsqrt(ssq_ref[...], d_model)


def _row_scale(ssq, d_model, tm=2048):
    t = ssq.shape[0]
    tm = _tile(t, tm)
    spec = pl.BlockSpec((tm, LANES), lambda i: (i, 0))
    return pl.pallas_call(
        functools.partial(_row_scale_kernel, d_model=d_model),
        grid=(t // tm,), in_specs=[spec], out_specs=spec,
        out_shape=jax.ShapeDtypeStruct(ssq.shape, F32),
        compiler_params=_params("parallel"),
        name="row_scale",
    )(ssq)


class _Pending(NamedTuple):
    w3: jax.Array
    layer: int
    g: jax.Array | None


def _rider_block(rows, cols, ni, nj):
    def split(size, parts, align):
        blk = _ceil_div(_ceil_div(size, parts), align) * align
        return blk, _ceil_div(size, blk)

    options = []
    for swap in (False, True):
        parts_r, parts_c = (nj, ni) if swap else (ni, nj)
        (bk, nbk), (bn, nbn) = split(rows, parts_r, BF16_ROWS), split(cols, parts_c, LANES)
        options.append((bk * bn, swap, bk, bn, nbk, nbn))
    _, swap, bk, bn, nbk, nbn = min(options)

    def index_map(i, j):
        ri, ci = (j, i) if swap else (i, j)
        return jnp.minimum(ri, nbk - 1), jnp.minimum(ci, nbn - 1)

    return (bk, bn), index_map


def _attach_riders(pending, ni, nj):
    operands, in_specs, out_specs, out_shape, scaled = [], [], [], [], []
    for p in pending:
        (bk, bn), index_map = _rider_block(*p.w3.shape[1:], ni, nj)
        operands.append(p.w3)
        in_specs.append(pl.BlockSpec((None, bk, bn),
                                     lambda i, j, p=p, index_map=index_map: (p.layer, *index_map(i, j))))
        if p.g is not None:
            operands.append(p.g.reshape(-1, 1))
            in_specs.append(pl.BlockSpec((bk, 1), lambda i, j, index_map=index_map: (index_map(i, j)[0], 0)))
        scaled.append(p.g is not None)
        out_specs.append(pl.BlockSpec((bk, bn), index_map))
        out_shape.append(jax.ShapeDtypeStruct(p.w3.shape[1:], BF16))
    return operands, in_specs, out_specs, out_shape, tuple(scaled)


def _run_riders(in_refs, out_refs, scaled):
    in_refs = iter(in_refs)
    for o_ref, has_g in zip(out_refs, scaled):
        w = next(in_refs)[...]
        if has_g:
            w = w * next(in_refs)[...]
        o_ref[...] = w.astype(o_ref.dtype)


def _n_rider_inputs(rider_scaled):
    return len(rider_scaled) + sum(rider_scaled)


def _mm_gelu_kernel(a_ref, r_ref, b_ref, *refs, rider_scaled):
    n_in = _n_rider_inputs(rider_scaled)
    rider_in, o_ref, rider_out = refs[:n_in], refs[n_in], refs[n_in + 1:]
    _run_riders(rider_in, rider_out, rider_scaled)
    r = r_ref[...]
    for cols in _slices(o_ref.shape[1], V7X_MXU_DIM):
        acc = _scale_rows(_dot(a_ref[...], b_ref[:, cols]), r)
        o_ref[:, cols] = jax.nn.gelu(acc).astype(o_ref.dtype)


def _mm_gelu(a, r, b, col_block_offset, n, out_dtype, pending=(), tm=1024, tn=1024):
    t, k = a.shape
    tm, tn = _tile(t, tm), _tile(n, tn)
    ni, nj = t // tm, n // tn
    r_ops, r_in, r_out, r_shape, scaled = _attach_riders(pending, ni, nj)
    out, *prepared = pl.pallas_call(
        functools.partial(_mm_gelu_kernel, rider_scaled=scaled),
        grid=(ni, nj),
        in_specs=[pl.BlockSpec((tm, k), lambda i, j: (i, 0)),
                  pl.BlockSpec((tm, LANES), lambda i, j: (i, 0)),
                  pl.BlockSpec((k, tn), lambda i, j: (0, j + col_block_offset))] + r_in,
        out_specs=[pl.BlockSpec((tm, tn), lambda i, j: (i, j))] + r_out,
        out_shape=[jax.ShapeDtypeStruct((t, n), out_dtype)] + r_shape,
        compiler_params=_params("parallel", "arbitrary"),
        name="mm_gelu",
    )(a, r, b, *r_ops)
    return out, prepared


_ROW_JOBS = {"cast": (_cast_ssq_kernel, 1, 2), "norm": (_final_norm_kernel, 3, 1)}


def _attach_row_job(job, ni, nj):
    if job is None:
        return [], [], [], [], None
    kind, x, *rest = job
    t, d = x.shape
    rb = _ceil_div(_ceil_div(t, ni * nj), BF16_ROWS) * BF16_ROWS
    last = _ceil_div(t, rb) - 1
    row = lambda i, j: (jnp.minimum(i * nj + j, last), 0)
    wide, lanes = pl.BlockSpec((rb, d), row), pl.BlockSpec((rb, LANES), row)
    if kind == "cast":
        return ([x], [wide], [wide, lanes],
                [jax.ShapeDtypeStruct((t, d), BF16), jax.ShapeDtypeStruct((t, LANES), F32)], kind)
    ssq, g = rest
    return ([x, ssq, g.reshape(1, d)], [wide, lanes, pl.BlockSpec((1, d), lambda i, j: (0, 0))],
            [wide], [jax.ShapeDtypeStruct((t, d), F32)], kind)


def _mm_swiglu_kernel(a_ref, ssq_ref, wg_ref, wu_ref, *refs, rider_scaled, row_job):
    n_rider_in = _n_rider_inputs(rider_scaled)
    body, n_job_in, n_job_out = _ROW_JOBS[row_job] if row_job else (None, 0, 0)
    rider_in, refs = refs[:n_rider_in], refs[n_rider_in:]
    job_in, refs = refs[:n_job_in], refs[n_job_in:]
    o_ref, *outs, r_ref = refs
    rider_out, job_out = outs[:len(rider_scaled)], outs[len(rider_scaled):]
    assert len(job_out) == n_job_out
    _run_riders(rider_in, rider_out, rider_scaled)
    if body is not None:
        body(*job_in, *job_out)
    r_ref[...] = _row_rsqrt(ssq_ref[...], a_ref.shape[1])
    for rows, cols in _chunks(o_ref.shape, rows=V7X_MXU_DIM):
        a = a_ref[rows, :]
        r = r_ref[rows, :]
        g = _scale_rows(_dot(a, wg_ref[:, cols]), r)
        u = _scale_rows(_dot(a, wu_ref[:, cols]), r)
        o_ref[rows, cols] = (g * jax.nn.sigmoid(g) * u).astype(o_ref.dtype)


def _mm_swiglu(a, ssq, wg, wu, pending=(), row_job=None, tm=2048, tn=V7X_MXU_DIM):
    t, k = a.shape
    n = wg.shape[1]
    tm, tn = _tile(t, tm), _tile(n, tn)
    ni, nj = t // tm, n // tn
    w_spec = pl.BlockSpec((k, tn), lambda i, j: (0, j))
    r_ops, r_in, r_out, r_shape, scaled = _attach_riders(pending, ni, nj)
    j_ops, j_in, j_out, j_shape, job_kind = _attach_row_job(row_job, ni, nj)
    hidden, *extra = pl.pallas_call(
        functools.partial(_mm_swiglu_kernel, rider_scaled=scaled, row_job=job_kind),
        grid=(ni, nj),
        in_specs=[pl.BlockSpec((tm, k), lambda i, j: (i, 0)),
                  pl.BlockSpec((tm, LANES), lambda i, j: (i, 0)), w_spec, w_spec] + r_in + j_in,
        out_specs=[pl.BlockSpec((tm, tn), lambda i, j: (i, j))] + r_out + j_out,
        out_shape=[jax.ShapeDtypeStruct((t, n), BF16)] + r_shape + j_shape,
        scratch_shapes=[pltpu.VMEM((tm, LANES), F32)],
        compiler_params=_params("arbitrary" if row_job else "parallel", "arbitrary"),
        name="mm_swiglu",
    )(a, ssq, wg, wu, *r_ops, *j_ops)
    return hidden, extra[:len(pending)], extra[len(pending):]


def _mm_residual_kernel(a_ref, b_ref, res_ref, *refs, emit_norm, rider_scaled):
    n_in = _n_rider_inputs(rider_scaled)
    n_out = 3 if emit_norm else 1
    rider_in, o_ref, rider_out = refs[:n_in], refs[n_in], refs[n_in + n_out:]
    _run_riders(rider_in, rider_out, rider_scaled)
    if emit_norm:
        xb_ref, ssq_ref = refs[n_in + 1:n_in + 3]

        @pl.when(pl.program_id(1) == 0)
        def _():
            ssq_ref[...] = jnp.zeros_like(ssq_ref)

    for rows, cols in _chunks(o_ref.shape):
        x_new = res_ref[rows, cols] + _dot(a_ref[rows, :], b_ref[:, cols])
        o_ref[rows, cols] = x_new
        if emit_norm:
            xb_ref[rows, cols] = x_new.astype(xb_ref.dtype)
            ssq_ref[rows, :] += _lane_partial_ssq(x_new)


def _mm_residual(a, b, res, k_start=0, k_size=None, emit_norm=False, pending=()):
    t, k = a.shape
    n = b.shape[1]
    k_size = k if k_size is None else k_size
    tm, tn = (2048, 256) if 2 * 2048 * k_size * 2 <= V7X_VMEM_BYTES // 2 else (1024, 512)
    tm, tn = _tile(t, tm), _tile(n, tn)
    ni, nj = t // tm, n // tn
    tile = pl.BlockSpec((tm, tn), lambda i, j: (i, j))
    out_specs = [tile]
    out_shape = [jax.ShapeDtypeStruct((t, n), F32)]
    if emit_norm:
        out_specs += [tile, pl.BlockSpec((tm, LANES), lambda i, j: (i, 0))]
        out_shape += [jax.ShapeDtypeStruct((t, n), BF16), jax.ShapeDtypeStruct((t, LANES), F32)]
    r_ops, r_in, r_out, r_shape, scaled = _attach_riders(pending, ni, nj)
    out = pl.pallas_call(
        functools.partial(_mm_residual_kernel, emit_norm=emit_norm, rider_scaled=scaled),
        grid=(ni, nj),
        in_specs=[pl.BlockSpec((pl.Element(tm), pl.Element(k_size)), lambda i, j: (i * tm, k_start)),
                  pl.BlockSpec((pl.Element(k_size), pl.Element(tn)), lambda i, j: (k_start, j * tn)),
                  tile] + r_in,
        out_specs=out_specs + r_out,
        out_shape=out_shape + r_shape,
        compiler_params=_params("parallel", "arbitrary"),
        name="mm_residual_norm" if emit_norm else "mm_residual",
    )(a, b, res, *r_ops)
    n_out = len(out_shape)
    return (*out[:n_out], out[n_out:])


def _sgu_kernel(u_ref, v_ref, g_ref, b_ref, ws_ref, bias_ref, o_ref, vn_ref, *,
                chunk, heads):
    v = v_ref[...]
    mu = jnp.mean(v, axis=-1, keepdims=True)
    vc = v - mu
    r = lax.rsqrt(jnp.mean(vc * vc, axis=-1, keepdims=True) + EPS)
    vn_ref[...] = (vc * r * g_ref[...] + b_ref[...]).astype(vn_ref.dtype)
    hd = v.shape[1] // heads
    for c in range(v.shape[0] // chunk):
        rows = pl.ds(c * chunk, chunk)
        for h in range(heads):
            cols = pl.ds(h * hd, hd)
            s = _dot(ws_ref[h], vn_ref[rows, cols]) + bias_ref[:, cols]
            o_ref[rows, cols] = (u_ref[rows, cols].astype(F32) * s).astype(o_ref.dtype)


def _sgu(u, v, ln_g, ln_b, w_s, b_s, tm=512):
    t, d = v.shape
    heads, chunk, _ = w_s.shape
    bias = jnp.repeat(jnp.transpose(b_s), d // heads, axis=1)
    row = lambda i: (i, 0)
    fixed2 = lambda i: (0, 0)
    return pl.pallas_call(
        functools.partial(_sgu_kernel, chunk=chunk, heads=heads),
        grid=(t // tm,),
        in_specs=[pl.BlockSpec((tm, d), row),
                  pl.BlockSpec((tm, d), row),
                  pl.BlockSpec((1, d), fixed2),
                  pl.BlockSpec((1, d), fixed2),
                  pl.BlockSpec((heads, chunk, chunk), lambda i: (0, 0, 0)),
                  pl.BlockSpec((chunk, d), fixed2)],
        out_specs=pl.BlockSpec((tm, d), row),
        out_shape=jax.ShapeDtypeStruct((t, d), BF16),
        scratch_shapes=[pltpu.VMEM((tm, d), BF16)],
        compiler_params=_params("parallel"),
        name="sgu",
    )(u, v, ln_g.reshape(1, d), ln_b.reshape(1, d), w_s.astype(BF16), bias)


def _dft_tables(n, cols=None, residues=1, split=64):
    cols = n if cols is None else cols
    split = min(split, n)
    j = jnp.arange(cols, dtype=jnp.int32)[None, :]
    unit = 2.0 * math.pi / n
    a_lo = ((jnp.arange(split, dtype=jnp.int32)[:, None] * j) % n).astype(F32) * unit
    a_hi = (((jnp.arange(n // split, dtype=jnp.int32) * split)[:, None] * j) % n).astype(F32) * unit
    scale = 1.0 / math.sqrt(n)

    def lo(t):
        return jnp.transpose(t.reshape(split // residues, residues, cols), (1, 0, 2))[:, None]

    c_lo, s_lo = lo(jnp.cos(a_lo)), lo(jnp.sin(a_lo))
    c_hi, s_hi = (jnp.cos(a_hi) * scale)[None, :, None, :], (jnp.sin(a_hi) * scale)[None, :, None, :]
    cos = (c_hi * c_lo - s_hi * s_lo).reshape(residues, n // residues, cols)
    sin = (s_hi * c_lo + c_hi * s_lo).reshape(residues, n // residues, cols)
    return cos, sin


def _position_tables(seq):
    cos, sin = _dft_tables(seq, cols=seq // DFT_RADIX, residues=DFT_RADIX)
    return jnp.concatenate([cos, -sin], axis=2).astype(BF16)


def _dft_channels_kernel(h_ref, ssq_ref, cs_ref, pq_ref, r_ref):
    radix, tm, d = h_ref.shape
    groups, gd, _ = cs_ref.shape
    width = min(V7X_MXU_DIM, gd)
    for b in range(radix):
        r_ref[b] = _row_rsqrt(ssq_ref[b], d)
    granules = tm // 2 // BF16_ROWS
    for t0 in (0, tm // 2):
        for g in range(groups):
            gcols = slice(g * gd, (g + 1) * gd)
            lhs = jnp.concatenate(
                [h_ref[b, t0 + u * BF16_ROWS:t0 + (u + 1) * BF16_ROWS, gcols]
                 for u in range(granules) for b in range(radix)], axis=0)
            for si in range(gd // width):
                acc = _dot(lhs, cs_ref[g, :, 2 * si * width:2 * (si + 1) * width])
                cols = slice(g * gd + si * width, g * gd + (si + 1) * width)
                for u in range(granules):
                    rows = slice(t0 + u * BF16_ROWS, t0 + (u + 1) * BF16_ROWS)
                    p, q = [], []
                    for b in range(radix):
                        lo = (u * radix + b) * BF16_ROWS
                        blk = _scale_rows(acc[lo:lo + BF16_ROWS], r_ref[b, rows, :])
                        p.append(blk[:, :width])
                        q.append(blk[:, width:])
                    pa, pb, pc, pd = p[0] + p[2], p[1] + p[3], p[0] - p[2], p[1] - p[3]
                    qa, qb, qc, qd = q[0] + q[2], q[1] + q[3], q[0] - q[2], q[1] - q[3]
                    out = [(pa + pb, qa + qb), (pc - qd, qc + pd), (pa - pb, qa - qb), (pc + qd, qc - pd)]
                    for k, (vp, vq) in enumerate(out):
                        pq_ref[k, 0, rows, cols] = vp.astype(pq_ref.dtype)
                        pq_ref[k, 1, rows, cols] = vq.astype(pq_ref.dtype)


def _dft_channels(h, ssq, cs, batch, tm=128):
    t, d = h.shape
    seq = t // batch
    quarter = seq // DFT_RADIX
    tm = _tile(quarter, tm)
    return pl.pallas_call(
        _dft_channels_kernel,
        grid=(batch, quarter // tm),
        in_specs=[pl.BlockSpec((None, DFT_RADIX, tm, d), lambda b, i: (b, 0, i, 0)),
                  pl.BlockSpec((None, DFT_RADIX, tm, LANES), lambda b, i: (b, 0, i, 0)),
                  pl.BlockSpec(cs.shape, lambda b, i: (0, 0, 0))],
        out_specs=pl.BlockSpec((None, DFT_RADIX, 2, tm, d), lambda b, i: (b, 0, 0, i, 0)),
        out_shape=jax.ShapeDtypeStruct((batch, DFT_RADIX, 2, quarter, d), BF16),
        scratch_shapes=[pltpu.VMEM((DFT_RADIX, tm, LANES), F32)],
        compiler_params=_params("parallel", "parallel"),
        name="dft_channels",
    )(h.reshape(batch, DFT_RADIX, quarter, d), ssq.reshape(batch, DFT_RADIX, quarter, LANES), cs)


def _dft_positions_kernel(trig_ref, pq_ref, o_ref, il_ref):
    radix, quarter, _ = trig_ref.shape
    for ci, cols in enumerate(_slices(o_ref.shape[1], V7X_MXU_DIM)):
        il = il_ref.at[ci % 2]
        lane_groups = _slices(cols.stop - cols.start, LANES)
        for k in range(radix):
            for rows in _slices(quarter, 2 * V7X_MXU_DIM):
                acc = _dot(trig_ref[k, rows, :], pq_ref[k, :, cols])
                dst = pl.ds(radix * rows.start + k, rows.stop - rows.start, stride=radix)
                for c, lanes in enumerate(lane_groups):
                    il[c, dst, :] = acc[:, lanes]
        for c, lanes in enumerate(lane_groups):
            o_ref[:, cols.start + lanes.start:cols.start + lanes.stop] = il[c].astype(o_ref.dtype)


def _dft_positions(trig, pq):
    batch, radix, k2, d = pq.shape
    seq = radix * k2 // 2
    tn = _tile(d, (8 * 1024 * 1024) // (2 * radix * k2))
    return pl.pallas_call(
        _dft_positions_kernel,
        grid=(batch, d // tn),
        in_specs=[pl.BlockSpec(trig.shape, lambda b, j: (0, 0, 0), pipeline_mode=pl.Buffered(1)),
                  pl.BlockSpec((None, radix, k2, tn), lambda b, j: (b, 0, 0, j))],
        out_specs=pl.BlockSpec((None, seq, tn), lambda b, j: (b, 0, j)),
        out_shape=jax.ShapeDtypeStruct((batch, seq, d), BF16),
        scratch_shapes=[pltpu.VMEM((2, V7X_MXU_DIM // LANES, seq, LANES), F32)],
        compiler_params=_params("parallel", "arbitrary"),
        name="dft_positions",
    )(trig, pq)


def _resolve(weights, name):
    if isinstance(weights[name], _Pending):
        weights[name] = _prep_weight(*weights[name])
    return weights[name]


def _take_pending(weights, names):
    names = [n for n in names if weights is not None and isinstance(weights.get(n), _Pending)]
    return names, [weights[n] for n in names]


def _ffn(x, xb, ssq, layer, next_layer, row_job=None):
    w_gate, w_up, w_down = (_resolve(layer, n) for n in ("w_gate", "w_up", "w_down"))
    names, pending = _take_pending(next_layer, ("w_gate", "w_up", "w_down", "w_out"))
    hidden, prepared, job_out = _mm_swiglu(xb, ssq, w_gate, w_up, pending, row_job)
    if names:
        next_layer.update(zip(names, prepared))
    k = w_down.shape[0]
    k_lo = _ceil_div(_ceil_div(k, V7X_MXU_DIM), 2) * V7X_MXU_DIM
    x_mid, _ = _mm_residual(hidden, w_down, x, k_start=0, k_size=k_lo)
    x, xb, ssq, _ = _mm_residual(hidden, w_down, x_mid, k_start=k_lo, k_size=k - k_lo, emit_norm=True)
    return x, xb, ssq, job_out


def _gmlp_layer(x, xb, ssq, layer):
    w_in = _resolve(layer, "w_in")
    d_u = layer["w_out"].w3.shape[1] if isinstance(layer["w_out"], _Pending) else layer["w_out"].shape[0]
    tn = _tile(d_u, 1024)
    r = _row_scale(ssq, x.shape[1])
    names, pending = _take_pending(layer, ("w_gate", "w_out"))
    u, prepared = _mm_gelu(xb, r, w_in, 0, d_u, BF16, pending, tn=tn)
    layer.update(zip(names, prepared))
    names, pending = _take_pending(layer, ("w_up",))
    v, prepared = _mm_gelu(xb, r, w_in, d_u // tn, d_u, F32, pending, tn=tn)
    layer.update(zip(names, prepared))
    gated = _sgu(u, v, layer["ln_g"], layer["ln_b"], layer["w_s"], layer["b_s"])
    names, pending = _take_pending(layer, ("w_down",))
    x, xb, ssq, prepared = _mm_residual(gated, layer["w_out"], x, emit_norm=True, pending=pending)
    layer.update(zip(names, prepared))
    return x, xb, ssq


def _fourier_layer(x, xb, ssq, batch, layer):
    t, d = x.shape
    seq = t // batch
    quarter = seq // DFT_RADIX
    cs = _resolve(layer, "cs").reshape(B_GROUPS, d // B_GROUPS, 2 * d // B_GROUPS)
    pq = _dft_channels(xb, ssq, cs, batch)
    y = _dft_positions(_position_tables(seq), pq.reshape(batch, DFT_RADIX, 2 * quarter, d))
    x, xb, ssq, _ = _mm_residual(y.reshape(t, d), _resolve(layer, "w_out"), x, emit_norm=True)
    return x, xb, ssq


def _trunk(x, xb, ssq, batch, layers, row_job):
    job_out = None
    for i, layer in enumerate(layers):
        if i % 2 == 0:
            x, xb, ssq = _gmlp_layer(x, xb, ssq, layer)
        else:
            x, xb, ssq = _fourier_layer(x, xb, ssq, batch, layer)
        next_layer = layers[i + 1] if i + 1 < len(layers) else None
        x, xb, ssq, out = _ffn(x, xb, ssq, layer, next_layer, row_job if i == 0 else None)
        job_out = out if i == 0 else job_out
    return x, ssq, job_out


def kernel(x_prompt, x_sample, a_norm_g, a_w_in, a_ln_g, a_ln_b, a_w_s, a_b_s, a_w_out,
           b_norm_g, b_w_out, ffn_norm_g, ffn_w_gate, ffn_w_up, ffn_w_down, final_norm_g):
    depth, d, _ = ffn_w_gate.shape
    gd = d // B_GROUPS
    width = min(V7X_MXU_DIM, gd)
    cs_pair = jnp.stack([t.reshape(gd, gd // width, width) for t in _dft_tables(gd)], axis=2)
    cs3 = jnp.tile(cs_pair.reshape(gd, 2 * gd), (B_GROUPS, 1))[None]
    layers = []
    for i in range(depth):
        j = i // 2
        layer = dict(w_gate=_Pending(ffn_w_gate, i, ffn_norm_g[i]),
                     w_up=_Pending(ffn_w_up, i, ffn_norm_g[i]),
                     w_down=_Pending(ffn_w_down, i, None))
        if i % 2 == 0:
            layer.update(w_in=_Pending(a_w_in, j, a_norm_g[j]), w_out=_Pending(a_w_out, j, None),
                         ln_g=a_ln_g[j], ln_b=a_ln_b[j], w_s=a_w_s[j], b_s=a_b_s[j])
        else:
            layer.update(cs=_Pending(cs3, 0, b_norm_g[j]), w_out=_Pending(b_w_out, j, None))
        layers.append(layer)
    xp, xs = x_prompt.reshape(-1, d), x_sample.reshape(-1, d)
    xbp, ssqp = _cast_ssq(xp)
    xp, ssqp, (xbs, ssqs) = _trunk(xp, xbp, ssqp, x_prompt.shape[0], layers, ("cast", xs))
    xs, ssqs, (yp,) = _trunk(xs, xbs, ssqs, x_sample.shape[0], layers, ("norm", xp, ssqp, final_norm_g))
    ys = _final_norm(xs, ssqs, final_norm_g)
    return yp.reshape(x_prompt.shape), ys.reshape(x_sample.shape)
```

```python
import functools
import math
from typing import NamedTuple

import jax
import jax.numpy as jnp
from jax import lax
from jax.experimental import pallas as pl
from jax.experimental.pallas import tpu as pltpu

EPS = 1e-6
B_GROUPS = 8
DFT_RADIX = 4

V7X_VMEM_BYTES = 64 * 1024 * 1024
VMEM_LIMIT_BYTES = V7X_VMEM_BYTES - 8 * 1024 * 1024
LANES = 128
BF16_ROWS = 16
V7X_MXU_DIM = 256

F32 = jnp.float32
BF16 = jnp.bfloat16


def _params(*semantics):
    return pltpu.CompilerParams(dimension_semantics=semantics,
                                vmem_limit_bytes=VMEM_LIMIT_BYTES)


def _dot(a, b):
    return jnp.dot(a, b, preferred_element_type=F32)


def _ceil_div(a, b):
    return -(-a // b)


def _tile(total, preferred):
    t = min(total, preferred)
    assert total % t == 0, (total, preferred)
    return t


def _slices(total, width):
    width = min(width, total)
    assert total % width == 0, (total, width)
    return [slice(c * width, (c + 1) * width) for c in range(total // width)]


def _chunks(shape, rows=2 * V7X_MXU_DIM, cols=V7X_MXU_DIM):
    return [(r, c) for c in _slices(shape[1], cols) for r in _slices(shape[0], rows)]


def _scale_rows(acc, r):
    return acc * jnp.tile(r, (1, acc.shape[1] // LANES))


def _row_rsqrt(ssq_lanes, d_model):
    tot = jnp.sum(ssq_lanes, axis=1, keepdims=True)
    return jnp.broadcast_to(lax.rsqrt(tot / d_model + EPS), ssq_lanes.shape)


def _lane_partial_ssq(x):
    sq = x * x
    part = sq[:, :LANES]
    for c in range(1, x.shape[1] // LANES):
        part = part + sq[:, c * LANES:(c + 1) * LANES]
    return part


def _prep_scaled_kernel(w_ref, g_ref, o_ref):
    o_ref[...] = (w_ref[...] * g_ref[...]).astype(o_ref.dtype)


def _prep_plain_kernel(w_ref, o_ref):
    o_ref[...] = w_ref[...].astype(o_ref.dtype)


def _prep_weight(w3, layer, g=None, tk=256):
    _, k, n = w3.shape
    tn = n
    while tn > 5504:
        tn //= 2
    assert n % tn == 0 and tn % LANES == 0 and k % tk == 0
    w_spec = pl.BlockSpec((None, tk, tn), lambda i, j: (layer, i, j))
    o_spec = pl.BlockSpec((tk, tn), lambda i, j: (i, j))
    common = dict(grid=(k // tk, n // tn), out_specs=o_spec,
                  out_shape=jax.ShapeDtypeStruct((k, n), BF16),
                  compiler_params=_params("parallel", "parallel"))
    if g is None:
        return pl.pallas_call(_prep_plain_kernel, in_specs=[w_spec], name="prep_plain", **common)(w3)
    g_spec = pl.BlockSpec((tk, 1), lambda i, j: (i, 0))
    return pl.pallas_call(_prep_scaled_kernel, in_specs=[w_spec, g_spec], name="prep_scaled",
                          **common)(w3, g.reshape(k, 1))


def _cast_ssq_kernel(x_ref, xb_ref, ssq_ref):
    x = x_ref[...]
    xb_ref[...] = x.astype(xb_ref.dtype)
    ssq_ref[...] = _lane_partial_ssq(x)


def _cast_ssq(x, tm=512):
    t, d = x.shape
    row = lambda i: (i, 0)
    return pl.pallas_call(
        _cast_ssq_kernel,
        grid=(t // tm,),
        in_specs=[pl.BlockSpec((tm, d), row)],
        out_specs=[pl.BlockSpec((tm, d), row), pl.BlockSpec((tm, LANES), row)],
        out_shape=[jax.ShapeDtypeStruct((t, d), BF16), jax.ShapeDtypeStruct((t, LANES), F32)],
        compiler_params=_params("parallel"),
        name="cast_ssq",
    )(x)


def _final_norm_kernel(x_ref, ssq_ref, g_ref, o_ref):
    x = x_ref[...]
    o_ref[...] = _scale_rows(x, _row_rsqrt(ssq_ref[...], x.shape[1])) * g_ref[...]


def _final_norm(x, ssq, g, tm=512):
    t, d = x.shape
    row = lambda i: (i, 0)
    return pl.pallas_call(
        _final_norm_kernel,
        grid=(t // tm,),
        in_specs=[pl.BlockSpec((tm, d), row), pl.BlockSpec((tm, LANES), row),
                  pl.BlockSpec((1, d), lambda i: (0, 0))],
        out_specs=pl.BlockSpec((tm, d), row),
        out_shape=jax.ShapeDtypeStruct((t, d), F32),
        compiler_params=_params("parallel"),
        name="final_norm",
    )(x, ssq, g.reshape(1, d))


def _row_scale_kernel(ssq_ref, r_ref, *, d_model):
    r_ref[...] = _row_rsqrt(ssq_ref[...], d_model)


def _row_scale(ssq, d_model, tm=2048):
    t = ssq.shape[0]
    tm = _tile(t, tm)
    spec = pl.BlockSpec((tm, LANES), lambda i: (i, 0))
    return pl.pallas_call(
        functools.partial(_row_scale_kernel, d_model=d_model),
        grid=(t // tm,), in_specs=[spec], out_specs=spec,
        out_shape=jax.ShapeDtypeStruct(ssq.shape, F32),
        compiler_params=_params("parallel"),
        name="row_scale",
    )(ssq)


class _Pending(NamedTuple):
    w3: jax.Array
    layer: int
    g: jax.Array | None


def _rider_block(rows, cols, ni, nj):
    def split(size, parts, align):
        blk = _ceil_div(_ceil_div(size, parts), align) * align
        return blk, _ceil_div(size, blk)

    options = []
    for swap in (False, True):
        parts_r, parts_c = (nj, ni) if swap else (ni, nj)
        (bk, nbk), (bn, nbn) = split(rows, parts_r, BF16_ROWS), split(cols, parts_c, LANES)
        options.append((bk * bn, swap, bk, bn, nbk, nbn))
    _, swap, bk, bn, nbk, nbn = min(options)

    def index_map(i, j):
        ri, ci = (j, i) if swap else (i, j)
        return jnp.minimum(ri, nbk - 1), jnp.minimum(ci, nbn - 1)

    return (bk, bn), index_map


def _attach_riders(pending, ni, nj):
    operands, in_specs, out_specs, out_shape, scaled = [], [], [], [], []
    for p in pending:
        (bk, bn), index_map = _rider_block(*p.w3.shape[1:], ni, nj)
        operands.append(p.w3)
        in_specs.append(pl.BlockSpec((None, bk, bn),
                                     lambda i, j, p=p, index_map=index_map: (p.layer, *index_map(i, j))))
        if p.g is not None:
            operands.append(p.g.reshape(-1, 1))
            in_specs.append(pl.BlockSpec((bk, 1), lambda i, j, index_map=index_map: (index_map(i, j)[0], 0)))
        scaled.append(p.g is not None)
        out_specs.append(pl.BlockSpec((bk, bn), index_map))
        out_shape.append(jax.ShapeDtypeStruct(p.w3.shape[1:], BF16))
    return operands, in_specs, out_specs, out_shape, tuple(scaled)


def _run_riders(in_refs, out_refs, scaled):
    in_refs = iter(in_refs)
    for o_ref, has_g in zip(out_refs, scaled):
        w = next(in_refs)[...]
        if has_g:
            w = w * next(in_refs)[...]
        o_ref[...] = w.astype(o_ref.dtype)


def _n_rider_inputs(rider_scaled):
    return len(rider_scaled) + sum(rider_scaled)


def _mm_gelu_kernel(a_ref, r_ref, b_ref, *refs, rider_scaled):
    n_in = _n_rider_inputs(rider_scaled)
    rider_in, o_ref, rider_out = refs[:n_in], refs[n_in], refs[n_in + 1:]
    _run_riders(rider_in, rider_out, rider_scaled)
    r = r_ref[...]
    for cols in _slices(o_ref.shape[1], V7X_MXU_DIM):
        acc = _scale_rows(_dot(a_ref[...], b_ref[:, cols]), r)
        o_ref[:, cols] = jax.nn.gelu(acc).astype(o_ref.dtype)


def _mm_gelu(a, r, b, col_block_offset, n, out_dtype, pending=(), tm=1024, tn=1024):
    t, k = a.shape
    tm, tn = _tile(t, tm), _tile(n, tn)
    ni, nj = t // tm, n // tn
    r_ops, r_in, r_out, r_shape, scaled = _attach_riders(pending, ni, nj)
    out, *prepared = pl.pallas_call(
        functools.partial(_mm_gelu_kernel, rider_scaled=scaled),
        grid=(ni, nj),
        in_specs=[pl.BlockSpec((tm, k), lambda i, j: (i, 0)),
                  pl.BlockSpec((tm, LANES), lambda i, j: (i, 0)),
                  pl.BlockSpec((k, tn), lambda i, j: (0, j + col_block_offset))] + r_in,
        out_specs=[pl.BlockSpec((tm, tn), lambda i, j: (i, j))] + r_out,
        out_shape=[jax.ShapeDtypeStruct((t, n), out_dtype)] + r_shape,
        compiler_params=_params("parallel", "arbitrary"),
        name="mm_gelu",
    )(a, r, b, *r_ops)
    return out, prepared


_ROW_JOBS = {"cast": (_cast_ssq_kernel, 1, 2), "norm": (_final_norm_kernel, 3, 1)}


def _attach_row_job(job, ni, nj):
    if job is None:
        return [], [], [], [], None
    kind, x, *rest = job
    t, d = x.shape
    rb = _ceil_div(_ceil_div(t, ni * nj), BF16_ROWS) * BF16_ROWS
    last = _ceil_div(t, rb) - 1
    row = lambda i, j: (jnp.minimum(i * nj + j, last), 0)
    wide, lanes = pl.BlockSpec((rb, d), row), pl.BlockSpec((rb, LANES), row)
    if kind == "cast":
        return ([x], [wide], [wide, lanes],
                [jax.ShapeDtypeStruct((t, d), BF16), jax.ShapeDtypeStruct((t, LANES), F32)], kind)
    ssq, g = rest
    return ([x, ssq, g.reshape(1, d)], [wide, lanes, pl.BlockSpec((1, d), lambda i, j: (0, 0))],
            [wide], [jax.ShapeDtypeStruct((t, d), F32)], kind)


def _mm_swiglu_kernel(a_ref, ssq_ref, wg_ref, wu_ref, *refs, rider_scaled, row_job):
    n_rider_in = _n_rider_inputs(rider_scaled)
    body, n_job_in, n_job_out = _ROW_JOBS[row_job] if row_job else (None, 0, 0)
    rider_in, refs = refs[:n_rider_in], refs[n_rider_in:]
    job_in, refs = refs[:n_job_in], refs[n_job_in:]
    o_ref, *outs, r_ref = refs
    rider_out, job_out = outs[:len(rider_scaled)], outs[len(rider_scaled):]
    assert len(job_out) == n_job_out
    _run_riders(rider_in, rider_out, rider_scaled)
    if body is not None:
        body(*job_in, *job_out)
    r_ref[...] = _row_rsqrt(ssq_ref[...], a_ref.shape[1])
    for rows, cols in _chunks(o_ref.shape, rows=V7X_MXU_DIM):
        a = a_ref[rows, :]
        r = r_ref[rows, :]
        g = _scale_rows(_dot(a, wg_ref[:, cols]), r)
        u = _scale_rows(_dot(a, wu_ref[:, cols]), r)
        o_ref[rows, cols] = (g * jax.nn.sigmoid(g) * u).astype(o_ref.dtype)


def _mm_swiglu(a, ssq, wg, wu, pending=(), row_job=None, tm=2048, tn=V7X_MXU_DIM):
    t, k = a.shape
    n = wg.shape[1]
    tm, tn = _tile(t, tm), _tile(n, tn)
    ni, nj = t // tm, n // tn
    w_spec = pl.BlockSpec((k, tn), lambda i, j: (0, j))
    r_ops, r_in, r_out, r_shape, scaled = _attach_riders(pending, ni, nj)
    j_ops, j_in, j_out, j_shape, job_kind = _attach_row_job(row_job, ni, nj)
    hidden, *extra = pl.pallas_call(
        functools.partial(_mm_swiglu_kernel, rider_scaled=scaled, row_job=job_kind),
        grid=(ni, nj),
        in_specs=[pl.BlockSpec((tm, k), lambda i, j: (i, 0)),
                  pl.BlockSpec((tm, LANES), lambda i, j: (i, 0)), w_spec, w_spec] + r_in + j_in,
        out_specs=[pl.BlockSpec((tm, tn), lambda i, j: (i, j))] + r_out + j_out,
        out_shape=[jax.ShapeDtypeStruct((t, n), BF16)] + r_shape + j_shape,
        scratch_shapes=[pltpu.VMEM((tm, LANES), F32)],
        compiler_params=_params("arbitrary" if row_job else "parallel", "arbitrary"),
        name="mm_swiglu",
    )(a, ssq, wg, wu, *r_ops, *j_ops)
    return hidden, extra[:len(pending)], extra[len(pending):]


def _mm_residual_kernel(a_hbm, b_ref, res_ref, *refs, emit_norm, rider_scaled, k_start):
    *refs, a_buf, a_sem = refs
    i, j = pl.program_id(0), pl.program_id(1)
    _, tm, k_size = a_buf.shape

    def a_copy(block):
        slot = block % 2
        return pltpu.make_async_copy(
            a_hbm.at[pl.ds(block * tm, tm), pl.ds(k_start, k_size)], a_buf.at[slot], a_sem.at[slot])

    @pl.when((i == 0) & (j == 0))
    def _():
        a_copy(0).start()

    @pl.when(j == 0)
    def _():
        a_copy(i).wait()

        @pl.when(i + 1 < pl.num_programs(0))
        def _():
            a_copy(i + 1).start()

    a_ref = a_buf.at[i % 2]
    n_in = _n_rider_inputs(rider_scaled)
    n_out = 3 if emit_norm else 1
    rider_in, o_ref, rider_out = refs[:n_in], refs[n_in], refs[n_in + n_out:]
    _run_riders(rider_in, rider_out, rider_scaled)
    if emit_norm:
        xb_ref, ssq_ref = refs[n_in + 1:n_in + 3]

        @pl.when(pl.program_id(1) == 0)
        def _():
            ssq_ref[...] = jnp.zeros_like(ssq_ref)

    for rows, cols in _chunks(o_ref.shape):
        x_new = res_ref[rows, cols] + _dot(a_ref[rows, :], b_ref[:, cols])
        o_ref[rows, cols] = x_new
        if emit_norm:
            xb_ref[rows, cols] = x_new.astype(xb_ref.dtype)
            ssq_ref[rows, :] += _lane_partial_ssq(x_new)


def _mm_residual(a, b, res, k_start=0, k_size=None, emit_norm=False, pending=()):
    t, k = a.shape
    n = b.shape[1]
    k_size = k if k_size is None else k_size
    tm, tn = (2048, 256) if 2 * 2048 * k_size * 2 <= V7X_VMEM_BYTES // 2 else (1024, 512)
    tm, tn = _tile(t, tm), _tile(n, tn)
    ni, nj = t // tm, n // tn
    tile = pl.BlockSpec((tm, tn), lambda i, j: (i, j))
    out_specs = [tile]
    out_shape = [jax.ShapeDtypeStruct((t, n), F32)]
    if emit_norm:
        out_specs += [tile, pl.BlockSpec((tm, LANES), lambda i, j: (i, 0))]
        out_shape += [jax.ShapeDtypeStruct((t, n), BF16), jax.ShapeDtypeStruct((t, LANES), F32)]
    r_ops, r_in, r_out, r_shape, scaled = _attach_riders(pending, ni, nj)
    out = pl.pallas_call(
        functools.partial(_mm_residual_kernel, emit_norm=emit_norm, rider_scaled=scaled,
                          k_start=k_start),
        grid=(ni, nj),
        in_specs=[pl.BlockSpec(memory_space=pl.ANY),
                  pl.BlockSpec((pl.Element(k_size), pl.Element(tn)), lambda i, j: (k_start, j * tn)),
                  tile] + r_in,
        out_specs=out_specs + r_out,
        out_shape=out_shape + r_shape,
        scratch_shapes=[pltpu.VMEM((2, tm, k_size), BF16), pltpu.SemaphoreType.DMA((2,))],
        compiler_params=_params("arbitrary", "arbitrary"),
        name="mm_residual_norm" if emit_norm else "mm_residual",
    )(a, b, res, *r_ops)
    n_out = len(out_shape)
    return (*out[:n_out], out[n_out:])


def _sgu_kernel(u_ref, v_ref, g_ref, b_ref, ws_ref, bias_ref, o_ref, vn_ref, *,
                chunk, heads):
    v = v_ref[...]
    mu = jnp.mean(v, axis=-1, keepdims=True)
    vc = v - mu
    r = lax.rsqrt(jnp.mean(vc * vc, axis=-1, keepdims=True) + EPS)
    vn_ref[...] = (vc * r * g_ref[...] + b_ref[...]).astype(vn_ref.dtype)
    hd = v.shape[1] // heads
    for c in range(v.shape[0] // chunk):
        rows = pl.ds(c * chunk, chunk)
        for h in range(heads):
            cols = pl.ds(h * hd, hd)
            s = _dot(ws_ref[h], vn_ref[rows, cols]) + bias_ref[:, cols]
            o_ref[rows, cols] = (u_ref[rows, cols].astype(F32) * s).astype(o_ref.dtype)


def _sgu(u, v, ln_g, ln_b, w_s, b_s, tm=512):
    t, d = v.shape
    heads, chunk, _ = w_s.shape
    bias = jnp.repeat(jnp.transpose(b_s), d // heads, axis=1)
    row = lambda i: (i, 0)
    fixed2 = lambda i: (0, 0)
    return pl.pallas_call(
        functools.partial(_sgu_kernel, chunk=chunk, heads=heads),
        grid=(t // tm,),
        in_specs=[pl.BlockSpec((tm, d), row),
                  pl.BlockSpec((tm, d), row),
                  pl.BlockSpec((1, d), fixed2),
                  pl.BlockSpec((1, d), fixed2),
                  pl.BlockSpec((heads, chunk, chunk), lambda i: (0, 0, 0)),
                  pl.BlockSpec((chunk, d), fixed2)],
        out_specs=pl.BlockSpec((tm, d), row),
        out_shape=jax.ShapeDtypeStruct((t, d), BF16),
        scratch_shapes=[pltpu.VMEM((tm, d), BF16)],
        compiler_params=_params("parallel"),
        name="sgu",
    )(u, v, ln_g.reshape(1, d), ln_b.reshape(1, d), w_s.astype(BF16), bias)


def _dft_tables(n, cols=None, residues=1, split=64):
    cols = n if cols is None else cols
    split = min(split, n)
    j = jnp.arange(cols, dtype=jnp.int32)[None, :]
    unit = 2.0 * math.pi / n
    a_lo = ((jnp.arange(split, dtype=jnp.int32)[:, None] * j) % n).astype(F32) * unit
    a_hi = (((jnp.arange(n // split, dtype=jnp.int32) * split)[:, None] * j) % n).astype(F32) * unit
    scale = 1.0 / math.sqrt(n)

    def lo(t):
        return jnp.transpose(t.reshape(split // residues, residues, cols), (1, 0, 2))[:, None]

    c_lo, s_lo = lo(jnp.cos(a_lo)), lo(jnp.sin(a_lo))
    c_hi, s_hi = (jnp.cos(a_hi) * scale)[None, :, None, :], (jnp.sin(a_hi) * scale)[None, :, None, :]
    cos = (c_hi * c_lo - s_hi * s_lo).reshape(residues, n // residues, cols)
    sin = (s_hi * c_lo + c_hi * s_lo).reshape(residues, n // residues, cols)
    return cos, sin


def _position_tables(seq):
    cos, sin = _dft_tables(seq, cols=seq // DFT_RADIX, residues=DFT_RADIX)
    return jnp.concatenate([cos, -sin], axis=2).astype(BF16)


def _dft_channels_kernel(h_ref, ssq_ref, cs_ref, pq_ref, r_ref):
    radix, tm, d = h_ref.shape
    groups, gd, _ = cs_ref.shape
    width = min(V7X_MXU_DIM, gd)
    for b in range(radix):
        r_ref[b] = _row_rsqrt(ssq_ref[b], d)
    granules = tm // 2 // BF16_ROWS
    for t0 in (0, tm // 2):
        for g in range(groups):
            gcols = slice(g * gd, (g + 1) * gd)
            lhs = jnp.concatenate(
                [h_ref[b, t0 + u * BF16_ROWS:t0 + (u + 1) * BF16_ROWS, gcols]
                 for u in range(granules) for b in range(radix)], axis=0)
            for si in range(gd // width):
                acc = _dot(lhs, cs_ref[g, :, 2 * si * width:2 * (si + 1) * width])
                cols = slice(g * gd + si * width, g * gd + (si + 1) * width)
                for u in range(granules):
                    rows = slice(t0 + u * BF16_ROWS, t0 + (u + 1) * BF16_ROWS)
                    p, q = [], []
                    for b in range(radix):
                        lo = (u * radix + b) * BF16_ROWS
                        blk = _scale_rows(acc[lo:lo + BF16_ROWS], r_ref[b, rows, :])
                        p.append(blk[:, :width])
                        q.append(blk[:, width:])
                    pa, pb, pc, pd = p[0] + p[2], p[1] + p[3], p[0] - p[2], p[1] - p[3]
                    qa, qb, qc, qd = q[0] + q[2], q[1] + q[3], q[0] - q[2], q[1] - q[3]
                    out = [(pa + pb, qa + qb), (pc - qd, qc + pd), (pa - pb, qa - qb), (pc + qd, qc - pd)]
                    for k, (vp, vq) in enumerate(out):
                        pq_ref[k, 0, rows, cols] = vp.astype(pq_ref.dtype)
                        pq_ref[k, 1, rows, cols] = vq.astype(pq_ref.dtype)


def _dft_channels(h, ssq, cs, batch, tm=128):
    t, d = h.shape
    seq = t // batch
    quarter = seq // DFT_RADIX
    tm = _tile(quarter, tm)
    return pl.pallas_call(
        _dft_channels_kernel,
        grid=(batch, quarter // tm),
        in_specs=[pl.BlockSpec((None, DFT_RADIX, tm, d), lambda b, i: (b, 0, i, 0)),
                  pl.BlockSpec((None, DFT_RADIX, tm, LANES), lambda b, i: (b, 0, i, 0)),
                  pl.BlockSpec(cs.shape, lambda b, i: (0, 0, 0))],
        out_specs=pl.BlockSpec((None, DFT_RADIX, 2, tm, d), lambda b, i: (b, 0, 0, i, 0)),
        out_shape=jax.ShapeDtypeStruct((batch, DFT_RADIX, 2, quarter, d), BF16),
        scratch_shapes=[pltpu.VMEM((DFT_RADIX, tm, LANES), F32)],
        compiler_params=_params("parallel", "parallel"),
        name="dft_channels",
    )(h.reshape(batch, DFT_RADIX, quarter, d), ssq.reshape(batch, DFT_RADIX, quarter, LANES), cs)


def _dft_positions_kernel(trig_ref, pq_ref, o_ref, il_ref):
    radix, quarter, _ = trig_ref.shape
    for ci, cols in enumerate(_slices(o_ref.shape[1], V7X_MXU_DIM)):
        il = il_ref.at[ci % 2]
        lane_groups = _slices(cols.stop - cols.start, LANES)
        for k in range(radix):
            for rows in _slices(quarter, 2 * V7X_MXU_DIM):
                acc = _dot(trig_ref[k, rows, :], pq_ref[k, :, cols])
                dst = pl.ds(radix * rows.start + k, rows.stop - rows.start, stride=radix)
                for c, lanes in enumerate(lane_groups):
                    il[c, dst, :] = acc[:, lanes]
        for c, lanes in enumerate(lane_groups):
            o_ref[:, cols.start + lanes.start:cols.start + lanes.stop] = il[c].astype(o_ref.dtype)


def _dft_positions(trig, pq):
    batch, radix, k2, d = pq.shape
    seq = radix * k2 // 2
    tn = _tile(d, (8 * 1024 * 1024) // (2 * radix * k2))
    return pl.pallas_call(
        _dft_positions_kernel,
        grid=(batch, d // tn),
        in_specs=[pl.BlockSpec(trig.shape, lambda b, j: (0, 0, 0), pipeline_mode=pl.Buffered(1)),
                  pl.BlockSpec((None, radix, k2, tn), lambda b, j: (b, 0, 0, j))],
        out_specs=pl.BlockSpec((None, seq, tn), lambda b, j: (b, 0, j)),
        out_shape=jax.ShapeDtypeStruct((batch, seq, d), BF16),
        scratch_shapes=[pltpu.VMEM((2, V7X_MXU_DIM // LANES, seq, LANES), F32)],
        compiler_params=_params("parallel", "arbitrary"),
        name="dft_positions",
    )(trig, pq)


def _resolve(weights, name):
    if isinstance(weights[name], _Pending):
        weights[name] = _prep_weight(*weights[name])
    return weights[name]


def _take_pending(weights, names):
    names = [n for n in names if weights is not None and isinstance(weights.get(n), _Pending)]
    return names, [weights[n] for n in names]


def _ffn(x, xb, ssq, layer, next_layer, row_job=None):
    w_gate, w_up, w_down = (_resolve(layer, n) for n in ("w_gate", "w_up", "w_down"))
    names, pending = _take_pending(next_layer, ("w_gate", "w_up", "w_down", "w_out"))
    hidden, prepared, job_out = _mm_swiglu(xb, ssq, w_gate, w_up, pending, row_job)
    if names:
        next_layer.update(zip(names, prepared))
    k = w_down.shape[0]
    k_lo = _ceil_div(_ceil_div(k, V7X_MXU_DIM), 2) * V7X_MXU_DIM
    x_mid, _ = _mm_residual(hidden, w_down, x, k_start=0, k_size=k_lo)
    x, xb, ssq, _ = _mm_residual(hidden, w_down, x_mid, k_start=k_lo, k_size=k - k_lo, emit_norm=True)
    return x, xb, ssq, job_out


def _gmlp_layer(x, xb, ssq, layer):
    w_in = _resolve(layer, "w_in")
    d_u = layer["w_out"].w3.shape[1] if isinstance(layer["w_out"], _Pending) else layer["w_out"].shape[0]
    tn = _tile(d_u, 1024)
    r = _row_scale(ssq, x.shape[1])
    names, pending = _take_pending(layer, ("w_gate", "w_out"))
    u, prepared = _mm_gelu(xb, r, w_in, 0, d_u, BF16, pending, tn=tn)
    layer.update(zip(names, prepared))
    names, pending = _take_pending(layer, ("w_up",))
    v, prepared = _mm_gelu(xb, r, w_in, d_u // tn, d_u, F32, pending, tn=tn)
    layer.update(zip(names, prepared))
    gated = _sgu(u, v, layer["ln_g"], layer["ln_b"], layer["w_s"], layer["b_s"])
    names, pending = _take_pending(layer, ("w_down",))
    x, xb, ssq, prepared = _mm_residual(gated, layer["w_out"], x, emit_norm=True, pending=pending)
    layer.update(zip(names, prepared))
    return x, xb, ssq


def _fourier_layer(x, xb, ssq, batch, layer):
    t, d = x.shape
    seq = t // batch
    quarter = seq // DFT_RADIX
    cs = _resolve(layer, "cs").reshape(B_GROUPS, d // B_GROUPS, 2 * d // B_GROUPS)
    pq = _dft_channels(xb, ssq, cs, batch)
    y = _dft_positions(_position_tables(seq), pq.reshape(batch, DFT_RADIX, 2 * quarter, d))
    x, xb, ssq, _ = _mm_residual(y.reshape(t, d), _resolve(layer, "w_out"), x, emit_norm=True)
    return x, xb, ssq


def _trunk(x, xb, ssq, batch, layers, row_job):
    job_out = None
    for i, layer in enumerate(layers):
        if i % 2 == 0:
            x, xb, ssq = _gmlp_layer(x, xb, ssq, layer)
        else:
            x, xb, ssq = _fourier_layer(x, xb, ssq, batch, layer)
        next_layer = layers[i + 1] if i + 1 < len(layers) else None
        x, xb, ssq, out = _ffn(x, xb, ssq, layer, next_layer, row_job if i == 0 else None)
        job_out = out if i == 0 else job_out
    return x, ssq, job_out


def kernel(x_prompt, x_sample, a_norm_g, a_w_in, a_ln_g, a_ln_b, a_w_s, a_b_s, a_w_out,
           b_norm_g, b_w_out, ffn_norm_g, ffn_w_gate, ffn_w_up, ffn_w_down, final_norm_g):
    depth, d, _ = ffn_w_gate.shape
    gd = d // B_GROUPS
    width = min(V7X_MXU_DIM, gd)
    cs_pair = jnp.stack([t.reshape(gd, gd // width, width) for t in _dft_tables(gd)], axis=2)
    cs3 = jnp.tile(cs_pair.reshape(gd, 2 * gd), (B_GROUPS, 1))[None]
    layers = []
    for i in range(depth):
        j = i // 2
        layer = dict(w_gate=_Pending(ffn_w_gate, i, ffn_norm_g[i]),
                     w_up=_Pending(ffn_w_up, i, ffn_norm_g[i]),
                     w_down=_Pending(ffn_w_down, i, None))
        if i % 2 == 0:
            layer.update(w_in=_Pending(a_w_in, j, a_norm_g[j]), w_out=_Pending(a_w_out, j, None),
                         ln_g=a_ln_g[j], ln_b=a_ln_b[j], w_s=a_w_s[j], b_s=a_b_s[j])
        else:
            layer.update(cs=_Pending(cs3, 0, b_norm_g[j]), w_out=_Pending(b_w_out, j, None))
        layers.append(layer)
    xp, xs = x_prompt.reshape(-1, d), x_sample.reshape(-1, d)
    xbp, ssqp = _cast_ssq(xp)
    xp, ssqp, (xbs, ssqs) = _trunk(xp, xbp, ssqp, x_prompt.shape[0], layers, ("cast", xs))
    xs, ssqs, (yp,) = _trunk(xs, xbs, ssqs, x_sample.shape[0], layers, ("norm", xp, ssqp, final_norm_g))
    ys = _final_norm(xs, ssqs, final_norm_g)
    return yp.reshape(x_prompt.shape), ys.reshape(x_sample.shape)
```
